```python
import jax, jax.numpy as jnp
from jax import lax
import numpy as np

D_MODEL = 1024
BATCH = 2
SEQ = 8192
DEPTH = 1
DEC_BATCH = 128
DEC_SEQ = 4
PAST_LEN = 8192
PAGE_SIZE = 128

PLE_DIM = 256
ROPE_THETA = 500000.0
NORM_EPS = 1e-6
Q_BLOCK = 128
NEG_INF = -1e30
MLA_HEADS = 8
MLA_NOPE = 64
MLA_ROPE = 32
MLA_V = 64
MLA_KV_LORA = 256
MLA_CACHE_DIM = MLA_KV_LORA + MLA_ROPE
NSA_HEADS = 8
NSA_KV_GROUPS = 2
NSA_REP = NSA_HEADS // NSA_KV_GROUPS
NSA_HEAD_DIM = 64
NSA_ROT = NSA_HEAD_DIM // 4
CMP_BLOCK = 64
SEL_BLOCK = 64
SEL_TOPK = 16
WINDOW = 512
D_FF = -(-8 * D_MODEL // (3 * 256)) * 256
SZ_MLA_Q = MLA_HEADS * (MLA_NOPE + MLA_ROPE)
SZ_MLA_C = MLA_KV_LORA
SZ_MLA_R = MLA_ROPE
SZ_NSA_Q = NSA_HEADS * NSA_HEAD_DIM
SZ_NSA_KV = 2 * NSA_KV_GROUPS * NSA_HEAD_DIM
SZ_NSA_G = 3 * NSA_HEADS
SZ_MERGE = 2 * D_MODEL
IN_SPLITS = (SZ_MLA_Q, SZ_MLA_C, SZ_MLA_R, SZ_NSA_Q, SZ_NSA_KV, SZ_NSA_KV, SZ_NSA_KV, SZ_NSA_G, SZ_MERGE)
IN_COLS = sum(IN_SPLITS)

kernel_name = 'mla_nsa_gated_hybrid_decode_step'


def rmsnorm(x, g):
    xf = x.astype(jnp.float32)
    y = xf * lax.rsqrt(jnp.mean(xf * xf, axis=-1, keepdims=True) + NORM_EPS)
    return (y * g.astype(jnp.float32)).astype(x.dtype)


def rope(x, pos, rot_dim):
    half = rot_dim // 2
    inv = jnp.float32(ROPE_THETA) ** (-jnp.arange(half, dtype=jnp.float32) * 2.0 / rot_dim)
    ang = pos.astype(jnp.float32)[:, None] * inv[None, :]
    cos = jnp.cos(ang)[None, :, None, :]
    sin = jnp.sin(ang)[None, :, None, :]
    xr = x[..., :rot_dim].astype(jnp.float32)
    x1, x2 = xr[..., :half], xr[..., half:]
    rot = jnp.concatenate([x1 * cos - x2 * sin, x2 * cos + x1 * sin], axis=-1).astype(x.dtype)
    return jnp.concatenate([rot, x[..., rot_dim:]], axis=-1)


def masked_softmax(s, mask):
    s = jnp.where(mask, s, NEG_INF)
    m = jnp.max(s, axis=-1, keepdims=True)
    e = jnp.where(mask, jnp.exp(s - m), 0.0)
    return e / jnp.maximum(jnp.sum(e, axis=-1, keepdims=True), 1e-30)


def project(h, pos, w_in, kv_norm):
    B, T, _ = h.shape
    G, d = NSA_KV_GROUPS, NSA_HEAD_DIM
    z = h @ w_in
    splits = [int(s) for s in np.cumsum(IN_SPLITS)[:-1]]
    qa, ca, ra, qb, zc, zs, zw, gn, gm = jnp.split(z, splits, axis=-1)
    qa = qa.reshape(B, T, MLA_HEADS, MLA_NOPE + MLA_ROPE)
    q_nope = qa[..., :MLA_NOPE]
    q_pe = rope(qa[..., MLA_NOPE:], pos, MLA_ROPE)
    c = rmsnorm(ca, kv_norm)
    k_pe = rope(ra[:, :, None, :], pos, MLA_ROPE)[:, :, 0, :]
    mla_row = jnp.concatenate([c, k_pe], axis=-1)
    q_nsa = rope(qb.reshape(B, T, NSA_HEADS, d), pos, NSA_ROT)

    def kv_rows(zz):
        zz = zz.reshape(B, T, 2, G, d)
        return jnp.stack([rope(zz[:, :, 0], pos, NSA_ROT), zz[:, :, 1]], axis=2)

    g_nsa = jax.nn.sigmoid(gn.reshape(B, T, NSA_HEADS, 3))
    g_mla_b, g_nsa_b = jnp.split(jax.nn.sigmoid(gm), 2, axis=-1)
    return q_nope, q_pe, mla_row, q_nsa, kv_rows(zc), kv_rows(zs), kv_rows(zw), g_nsa, g_mla_b, g_nsa_b


def mla_prompt(q_nope, q_pe, mla_row, w_uk, w_uv):
    B, S = q_nope.shape[:2]
    c, k_pe = mla_row[..., :MLA_KV_LORA], mla_row[..., MLA_KV_LORA:]
    k_nope = jnp.einsum('bsc,chn->bshn', c, w_uk)
    v = jnp.einsum('bsc,chv->bshv', c, w_uv)
    k = jnp.concatenate([k_nope, jnp.broadcast_to(k_pe[:, :, None, :], (B, S, MLA_HEADS, MLA_ROPE))], axis=-1)
    q = jnp.concatenate([q_nope, q_pe], axis=-1)
    scale = (MLA_NOPE + MLA_ROPE) ** -0.5
    nb = S // Q_BLOCK
    qblk = q.reshape(B, nb, Q_BLOCK, MLA_HEADS, MLA_NOPE + MLA_ROPE).transpose(1, 0, 2, 3, 4)
    kpos = jnp.arange(S)

    def body(args):
        i, qi = args
        qpos = i * Q_BLOCK + jnp.arange(Q_BLOCK)
        mask = kpos[None, :] <= qpos[:, None]
        s = jnp.einsum('bqhd,bkhd->bhqk', qi, k).astype(jnp.float32) * scale
        p = masked_softmax(s, mask)
        return jnp.einsum('bhqk,bkhv->bqhv', p.astype(v.dtype), v)

    out = lax.map(body, (jnp.arange(nb), qblk))
    return out.transpose(1, 0, 2, 3, 4).reshape(B, S, MLA_HEADS * MLA_V)


def mla_sample(q_nope, q_pe, mla_all, qpos, w_uk, w_uv):
    B, T = q_nope.shape[:2]
    c_all, kpe_all = mla_all[..., :MLA_KV_LORA], mla_all[..., MLA_KV_LORA:]
    scale = (MLA_NOPE + MLA_ROPE) ** -0.5
    q_lat = jnp.einsum('bthn,chn->bthc', q_nope, w_uk)
    s = (jnp.einsum('bthc,blc->bhtl', q_lat, c_all).astype(jnp.float32)
         + jnp.einsum('bthr,blr->bhtl', q_pe, kpe_all).astype(jnp.float32)) * scale
    kpos = jnp.arange(mla_all.shape[1])
    p = masked_softmax(s, kpos[None, :] <= qpos[:, None])
    o_lat = jnp.einsum('bhtl,blc->bthc', p.astype(c_all.dtype), c_all)
    return jnp.einsum('bthc,chv->bthv', o_lat, w_uv).reshape(B, T, MLA_HEADS * MLA_V)


def nsa_keys(kv_c, kv_s, w_cmp_k, w_cmp_v):
    B, L = kv_c.shape[:2]
    G, d = NSA_KV_GROUPS, NSA_HEAD_DIM
    n_cmp = L // CMP_BLOCK
    blocks = kv_c[:, :n_cmp * CMP_BLOCK].reshape(B, n_cmp, CMP_BLOCK, 2, G, d)
    kc_pool = jnp.einsum('bnjgd,jd->bngd', blocks[:, :, :, 0], w_cmp_k)
    vc_pool = jnp.einsum('bnjgd,jd->bngd', blocks[:, :, :, 1], w_cmp_v)
    cmp_end = (jnp.arange(n_cmp) + 1) * CMP_BLOCK - 1
    n_sel = -(-L // SEL_BLOCK)
    sp = jnp.pad(kv_s, ((0, 0), (0, n_sel * SEL_BLOCK - L), (0, 0), (0, 0), (0, 0)))
    sp = sp.reshape(B, n_sel, SEL_BLOCK, 2, G, d).transpose(3, 0, 4, 1, 2, 5)
    return kc_pool, vc_pool, cmp_end, sp[0], sp[1]


def nsa_chunk(q, qpos, kc_pool, vc_pool, cmp_end, ks_blk, vs_blk, kw, vw, kwpos, gates):
    B, Tq = q.shape[:2]
    G, d = NSA_KV_GROUPS, NSA_HEAD_DIM
    scale = NSA_HEAD_DIM ** -0.5
    s_c = jnp.einsum('btgrd,bngd->bgrtn', q, kc_pool).astype(jnp.float32) * scale
    p_c = masked_softmax(s_c, cmp_end[None, :] <= qpos[:, None])
    o_c = jnp.einsum('bgrtn,bngd->btgrd', p_c.astype(vc_pool.dtype), vc_pool)
    n_cmp = kc_pool.shape[1]
    n_sel = ks_blk.shape[2]
    imp = jnp.pad(p_c.sum(axis=2), ((0, 0), (0, 0), (0, 0), (0, n_sel - n_cmp)))
    blk = jnp.arange(n_sel)[None, :]
    cur = (qpos // SEL_BLOCK)[:, None]
    forced = (blk == 0) | (blk == cur) | (blk == cur - 1)
    imp = jnp.where(forced, jnp.inf, imp)
    imp = jnp.where(blk <= cur, imp, -jnp.inf)
    k_eff = min(SEL_TOPK, n_sel)
    _, sel = lax.top_k(imp, k_eff)
    gather = jax.vmap(jax.vmap(lambda blocks, ids: blocks[ids]))
    ks_sel = gather(ks_blk, sel).reshape(B, G, Tq, k_eff * SEL_BLOCK, d)
    vs_sel = gather(vs_blk, sel).reshape(B, G, Tq, k_eff * SEL_BLOCK, d)
    kpos = (sel[..., None] * SEL_BLOCK + jnp.arange(SEL_BLOCK)).reshape(B, G, Tq, k_eff * SEL_BLOCK)
    s_s = jnp.einsum('btgrd,bgtnd->bgrtn', q, ks_sel).astype(jnp.float32) * scale
    p_s = masked_softmax(s_s, (kpos <= qpos[None, None, :, None])[:, :, None])
    o_s = jnp.einsum('bgrtn,bgtnd->btgrd', p_s.astype(vs_sel.dtype), vs_sel)
    m_w = (kwpos[None, :] <= qpos[:, None]) & (kwpos[None, :] > qpos[:, None] - WINDOW) & (kwpos[None, :] >= 0)
    s_w = jnp.einsum('btgrd,blgd->bgrtl', q, kw).astype(jnp.float32) * scale
    p_w = masked_softmax(s_w, m_w)
    o_w = jnp.einsum('bgrtl,blgd->btgrd', p_w.astype(vw.dtype), vw)
    return gates[..., 0:1] * o_c + gates[..., 1:2] * o_s + gates[..., 2:3] * o_w


def nsa_prompt(q, kv_c, kv_s, kv_w, gates, w_cmp_k, w_cmp_v):
    B, S = q.shape[:2]
    G, R, d = NSA_KV_GROUPS, NSA_REP, NSA_HEAD_DIM
    kc_pool, vc_pool, cmp_end, ks_blk, vs_blk = nsa_keys(kv_c, kv_s, w_cmp_k, w_cmp_v)
    kw_pad = jnp.pad(kv_w, ((0, 0), (WINDOW, 0), (0, 0), (0, 0), (0, 0)))
    nb = S // Q_BLOCK
    qb = q.reshape(B, nb, Q_BLOCK, G, R, d).transpose(1, 0, 2, 3, 4, 5)
    gb = gates.reshape(B, nb, Q_BLOCK, G, R, 3).transpose(1, 0, 2, 3, 4, 5)

    def body(args):
        i, qi, gi = args
        start = i * Q_BLOCK
        qpos = start + jnp.arange(Q_BLOCK)
        kwi = lax.dynamic_slice_in_dim(kw_pad, start, WINDOW + Q_BLOCK, axis=1)
        kwpos = start - WINDOW + jnp.arange(WINDOW + Q_BLOCK)
        return nsa_chunk(qi, qpos, kc_pool, vc_pool, cmp_end, ks_blk, vs_blk,
                         kwi[:, :, 0], kwi[:, :, 1], kwpos, gi)

    out = lax.map(body, (jnp.arange(nb), qb, gb))
    return out.transpose(1, 0, 2, 3, 4, 5).reshape(B, S, NSA_HEADS * d)


def mixer_tail(x, y_mla, y_nsa, g_mla_b, g_nsa_b, p, w_branch_mla, w_branch_nsa, w_out,
               ln_ffn, w_ffn_gate, w_ffn_up, w_ffn_down, ln_ple, w_ple_gate, w_ple_proj):
    merged = g_mla_b * (y_mla @ w_branch_mla) + g_nsa_b * (y_nsa @ w_branch_nsa)
    x = x + merged @ w_out
    h = rmsnorm(x, ln_ffn)
    x = x + (jax.nn.silu(h @ w_ffn_gate) * (h @ w_ffn_up)) @ w_ffn_down
    x = x + jax.nn.sigmoid(rmsnorm(x, ln_ple) @ w_ple_gate) * (p @ w_ple_proj)
    return x


def setup_inputs(seed: int = 0) -> dict:
    key = jax.random.key(seed)
    k = jax.random.split(key, 32)
    f32 = jnp.float32
    G, d = NSA_KV_GROUPS, NSA_HEAD_DIM
    n_pages = PAST_LEN // PAGE_SIZE
    n_used = DEC_BATCH * n_pages
    n_pool = n_used + n_used // 4
    win_buf = min(WINDOW, PAST_LEN)

    def nrm(kk, shape, scale=1.0):
        return scale * jax.random.normal(kk, shape, f32)

    def gain(kk, shape):
        return 1.0 + 0.05 * jax.random.normal(kk, shape, f32)

    return {
        'x_prompt': nrm(k[0], (BATCH, SEQ, D_MODEL)),
        'x_sample': nrm(k[1], (DEC_BATCH, DEC_SEQ, D_MODEL)),
        'cache_mla': nrm(k[2], (DEPTH, n_pool, PAGE_SIZE, MLA_CACHE_DIM)),
        'cache_nsa_cmp': nrm(k[3], (DEPTH, n_pool, PAGE_SIZE, 2, G, d)),
        'cache_nsa_slc': nrm(k[4], (DEPTH, n_pool, PAGE_SIZE, 2, G, d)),
        'state_nsa_win': nrm(k[5], (DEPTH, DEC_BATCH, win_buf, 2, G, d)),
        'page_table': jax.random.permutation(k[6], n_pool)[:n_used].reshape(DEC_BATCH, n_pages).astype(jnp.int32),
        'p_prompt': nrm(k[7], (DEPTH, BATCH, SEQ, PLE_DIM)),
        'p_sample': nrm(k[8], (DEPTH, DEC_BATCH, DEC_SEQ, PLE_DIM)),
        'ln_attn': gain(k[9], (DEPTH, D_MODEL)),
        'w_in': nrm(k[10], (DEPTH, D_MODEL, IN_COLS), D_MODEL ** -0.5),
        'mla_kv_norm': gain(k[11], (DEPTH, MLA_KV_LORA)),
        'mla_w_uk': nrm(k[12], (DEPTH, MLA_KV_LORA, MLA_HEADS, MLA_NOPE), MLA_KV_LORA ** -0.5),
        'mla_w_uv': nrm(k[13], (DEPTH, MLA_KV_LORA, MLA_HEADS, MLA_V), MLA_KV_LORA ** -0.5),
        'nsa_w_cmp_k': (1.0 + 0.1 * jax.random.normal(k[14], (DEPTH, CMP_BLOCK, d), f32)) / CMP_BLOCK,
        'nsa_w_cmp_v': (1.0 + 0.1 * jax.random.normal(k[15], (DEPTH, CMP_BLOCK, d), f32)) / CMP_BLOCK,
        'w_branch_mla': nrm(k[16], (DEPTH, MLA_HEADS * MLA_V, D_MODEL), (MLA_HEADS * MLA_V) ** -0.5),
        'w_branch_nsa': nrm(k[17], (DEPTH, NSA_HEADS * d, D_MODEL), (NSA_HEADS * d) ** -0.5),
        'w_out': nrm(k[18], (DEPTH, D_MODEL, D_MODEL), D_MODEL ** -0.5),
        'ln_ffn': gain(k[19], (DEPTH, D_MODEL)),
        'w_ffn_gate': nrm(k[20], (DEPTH, D_MODEL, D_FF), D_MODEL ** -0.5),
        'w_ffn_up': nrm(k[21], (DEPTH, D_MODEL, D_FF), D_MODEL ** -0.5),
        'w_ffn_down': nrm(k[22], (DEPTH, D_FF, D_MODEL), D_FF ** -0.5),
        'ln_ple': gain(k[23], (DEPTH, D_MODEL)),
        'w_ple_gate': nrm(k[24], (DEPTH, D_MODEL, D_MODEL), D_MODEL ** -0.5),
        'w_ple_proj': nrm(k[25], (DEPTH, PLE_DIM, D_MODEL), PLE_DIM ** -0.5),
        'ln_final': gain(k[26], (D_MODEL,)),
    }


def reference(x_prompt, x_sample, cache_mla, cache_nsa_cmp, cache_nsa_slc, state_nsa_win, page_table,
              p_prompt, p_sample, ln_attn, w_in, mla_kv_norm, mla_w_uk, mla_w_uv, nsa_w_cmp_k, nsa_w_cmp_v,
              w_branch_mla, w_branch_nsa, w_out, ln_ffn, w_ffn_gate, w_ffn_up, w_ffn_down,
              ln_ple, w_ple_gate, w_ple_proj, ln_final):
    B, S, _ = x_prompt.shape
    DB, T, _ = x_sample.shape
    G, R, d = NSA_KV_GROUPS, NSA_REP, NSA_HEAD_DIM
    n_pages = page_table.shape[1]
    past = n_pages * PAGE_SIZE
    wb = state_nsa_win.shape[2]
    pos_p = jnp.arange(S)
    pos_s = past + jnp.arange(T)
    xp, xs = x_prompt, x_sample
    mla_p, mla_s, cmp_p, cmp_s, slc_p, slc_s, win_p, win_s = [], [], [], [], [], [], [], []
    for i in range(DEPTH):
        tail_w = (w_branch_mla[i], w_branch_nsa[i], w_out[i], ln_ffn[i], w_ffn_gate[i], w_ffn_up[i],
                  w_ffn_down[i], ln_ple[i], w_ple_gate[i], w_ple_proj[i])
        hp = rmsnorm(xp, ln_attn[i])
        q_nope, q_pe, mla_row, q_nsa, kvc, kvs, kvw, g_nsa, g_a, g_b = project(hp, pos_p, w_in[i], mla_kv_norm[i])
        ya = mla_prompt(q_nope, q_pe, mla_row, mla_w_uk[i], mla_w_uv[i])
        yb = nsa_prompt(q_nsa, kvc, kvs, kvw, g_nsa, nsa_w_cmp_k[i], nsa_w_cmp_v[i])
        xp = mixer_tail(xp, ya, yb, g_a, g_b, p_prompt[i], *tail_w)
        mla_p.append(mla_row)
        cmp_p.append(kvc)
        slc_p.append(kvs)
        win_p.append(kvw[:, -min(WINDOW, S):])
        hs = rmsnorm(xs, ln_attn[i])
        q_nope, q_pe, mla_row, q_nsa, kvc, kvs, kvw, g_nsa, g_a, g_b = project(hs, pos_s, w_in[i], mla_kv_norm[i])
        mla_all = jnp.concatenate([cache_mla[i][page_table].reshape(DB, past, MLA_CACHE_DIM), mla_row], axis=1)
        ya = mla_sample(q_nope, q_pe, mla_all, pos_s, mla_w_uk[i], mla_w_uv[i])
        kvc_all = jnp.concatenate([cache_nsa_cmp[i][page_table].reshape(DB, past, 2, G, d), kvc], axis=1)
        kvs_all = jnp.concatenate([cache_nsa_slc[i][page_table].reshape(DB, past, 2, G, d), kvs], axis=1)
        win_all = jnp.concatenate([state_nsa_win[i], kvw], axis=1)
        kwpos = past - wb + jnp.arange(wb + T)
        kc_pool, vc_pool, cmp_end, ks_blk, vs_blk = nsa_keys(kvc_all, kvs_all, nsa_w_cmp_k[i], nsa_w_cmp_v[i])
        yb = nsa_chunk(q_nsa.reshape(DB, T, G, R, d), pos_s, kc_pool, vc_pool, cmp_end, ks_blk, vs_blk,
                       win_all[:, :, 0], win_all[:, :, 1], kwpos,
                       g_nsa.reshape(DB, T, G, R, 3)).reshape(DB, T, NSA_HEADS * d)
        xs = mixer_tail(xs, ya, yb, g_a, g_b, p_sample[i], *tail_w)
        mla_s.append(mla_row)
        cmp_s.append(kvc)
        slc_s.append(kvs)
        win_s.append(win_all[:, -min(WINDOW, past + T):])
    y_prompt = rmsnorm(xp, ln_final)
    y_sample = rmsnorm(xs, ln_final)
    new_mla_prompt = jnp.stack(mla_p)
    new_mla_sample = jnp.stack(mla_s)
    new_cmp_prompt = jnp.stack(cmp_p)
    new_cmp_sample = jnp.stack(cmp_s)
    new_slc_prompt = jnp.stack(slc_p)
    new_slc_sample = jnp.stack(slc_s)
    new_win_prompt = jnp.stack(win_p)
    new_win_sample = jnp.stack(win_s)
    return (y_prompt, y_sample, new_mla_prompt, new_mla_sample, new_cmp_prompt, new_cmp_sample,
            new_slc_prompt, new_slc_sample, new_win_prompt, new_win_sample)
```

```python
import functools

import jax
import jax.numpy as jnp
import numpy as np
from jax import lax
from jax.experimental import pallas as pl
from jax.experimental.pallas import tpu as pltpu

F32 = jnp.float32
BF16 = jnp.bfloat16

NORM_EPS = 1e-6
ROPE_THETA = 500000.0
NEG_INF = -1e30
PAGE_SIZE = 128
MLA_HEADS = 8
MLA_NOPE = 64
MLA_ROPE = 32
MLA_V = 64
MLA_KV_LORA = 256
MLA_ROW = MLA_KV_LORA + MLA_ROPE
NSA_HEADS = 8
NSA_GROUPS = 2
NSA_REP = NSA_HEADS // NSA_GROUPS
NSA_DIM = 64
NSA_ROT = NSA_DIM // 4
CMP_BLOCK = 64
SEL_TOPK = 16
WINDOW = 512
LANES = 128
VMEM_LIMIT = 60 * 1024 * 1024

SEG_QA = MLA_HEADS * LANES
SEG_C = MLA_KV_LORA
SEG_R = LANES
SEG_QB = NSA_HEADS * NSA_DIM
SEG_KV = 3 * 2 * NSA_GROUPS * NSA_DIM
SEG_GN = LANES
OFF_C = SEG_QA
OFF_R = OFF_C + SEG_C
OFF_QB = OFF_R + SEG_R
OFF_KV = OFF_QB + SEG_QB
OFF_GN = OFF_KV + SEG_KV
OFF_GM = OFF_GN + SEG_GN


def _params(sem):
    return pltpu.CompilerParams(dimension_semantics=sem, vmem_limit_bytes=VMEM_LIMIT)


def _resident(shape):
    nd = len(shape)
    return pl.BlockSpec(shape, lambda *_: (0,) * nd, pipeline_mode=pl.Buffered(1))


def _dot(a, b):
    return jnp.dot(a, b, preferred_element_type=F32)


def _dot_nt(a, b):
    return lax.dot_general(a, b, (((1,), (1,)), ((), ())), preferred_element_type=F32)


def _rms(x, g):
    return x * lax.rsqrt(jnp.mean(x * x, axis=-1, keepdims=True) + NORM_EPS) * g


def _proj_kernel(x_ref, ln_ref, w_ref, wup_ref, kvn_ref, cm_ref, sm_ref, cn_ref, sn_ref, wpool_ref,
                 qm_ref, km_ref, vm_ref, mla_ref, qn_ref, kvc_ref, kvs_ref, kvw_ref,
                 slc_ref, win_ref, pool_ref, gn_ref, ga_ref, gb_ref):
    tm = x_ref.shape[0]
    d_model = x_ref.shape[1]
    h = _rms(x_ref[...], ln_ref[...]).astype(BF16)
    lane = lax.broadcasted_iota(jnp.int32, (tm, LANES), 1)
    cm, sm, cn, sn = cm_ref[...], sm_ref[...], cn_ref[...], sn_ref[...]

    def rope_m(z):
        half = MLA_ROPE // 2
        sw = jnp.where(lane < MLA_NOPE + half, pltpu.roll(z, LANES - half, 1), pltpu.roll(z, half, 1))
        return z * cm + sw * sm

    def rope_n(z):
        half = NSA_ROT // 2
        sw = jnp.where((lane & (NSA_DIM - 1)) < half, pltpu.roll(z, LANES - half, 1), pltpu.roll(z, half, 1))
        return z * cn + sw * sn

    zq = _dot(h, w_ref[:, 0:SEG_QA])
    scale_m = (MLA_NOPE + MLA_ROPE) ** -0.5
    for hh in range(MLA_HEADS):
        qm_ref[0, hh] = (rope_m(zq[:, hh * LANES:(hh + 1) * LANES]) * scale_m).astype(BF16)

    zc = _dot(h, w_ref[:, OFF_C:OFF_R])
    c = _rms(zc, kvn_ref[...])
    kpe = rope_m(_dot(h, w_ref[:, OFF_R:OFF_QB]))
    mla_ref[:, 0:MLA_KV_LORA] = c
    mla_ref[:, MLA_KV_LORA:MLA_ROW] = kpe[:, MLA_NOPE:MLA_NOPE + MLA_ROPE]
    up = _dot(c.astype(BF16), wup_ref[...])
    for hh in range(MLA_HEADS):
        km_ref[0, hh] = (up[:, hh * LANES:(hh + 1) * LANES] + kpe).astype(BF16)
        vo = SEG_QA + hh * MLA_V
        vm_ref[0, hh] = up[:, vo:vo + MLA_V].astype(BF16)

    zqb = _dot(h, w_ref[:, OFF_QB:OFF_KV])
    zero_half = jnp.zeros((tm, NSA_DIM), F32)
    scale_n = NSA_DIM ** -0.5
    for pair in range(NSA_HEADS // 2):
        zz = rope_n(zqb[:, pair * LANES:(pair + 1) * LANES]) * scale_n
        qn_ref[0, 2 * pair] = jnp.concatenate([zz[:, :NSA_DIM], zero_half], axis=1).astype(BF16)
        qn_ref[0, 2 * pair + 1] = jnp.concatenate([zz[:, NSA_DIM:], zero_half], axis=1).astype(BF16)

    zkv = _dot(h, w_ref[:, OFF_KV:OFF_GN])
    width = 2 * NSA_GROUPS * NSA_DIM

    def kv_rows(i):
        z = zkv[:, i * width:(i + 1) * width]
        return jnp.concatenate([rope_n(z[:, :LANES]), z[:, LANES:]], axis=1)

    def packed(full, g):
        ko, vo = g * NSA_DIM, LANES + g * NSA_DIM
        return jnp.concatenate([full[:, ko:ko + NSA_DIM], full[:, vo:vo + NSA_DIM]], axis=1).astype(BF16)

    rows_c, rows_s, rows_w = kv_rows(0), kv_rows(1), kv_rows(2)
    kvc_ref[...] = rows_c
    kvs_ref[...] = rows_s
    kvw_ref[...] = rows_w
    for g in range(NSA_GROUPS):
        slc_ref[0, g] = packed(rows_s, g)
        win_ref[0, g] = packed(rows_w, g)
    pool_ref[...] = jnp.sum(rows_c.reshape(tm // CMP_BLOCK, CMP_BLOCK, width) * wpool_ref[...][None], axis=1)

    gn_ref[...] = jax.nn.sigmoid(_dot(h, w_ref[:, OFF_GN:OFF_GM]))
    zm = _dot(h, w_ref[:, OFF_GM:OFF_GM + 2 * d_model])
    ga_ref[...] = jax.nn.sigmoid(zm[:, :d_model])
    gb_ref[...] = jax.nn.sigmoid(zm[:, d_model:])


def _proj(x, tabs, ln, w_all, w_up, kv_norm, w_pool, nb, ns, tm):
    n, d_model = x.shape
    nt = ns // tm
    width = 2 * NSA_GROUPS * NSA_DIM
    row = lambda w: pl.BlockSpec((tm, w), lambda i: (i, 0))
    head = lambda hn, w: pl.BlockSpec((1, hn, tm, w), lambda i: (i // nt, 0, i % nt, 0))
    tab = pl.BlockSpec((tm, LANES), lambda i: (i % nt, 0))
    out_shape = (
        jax.ShapeDtypeStruct((nb, MLA_HEADS, ns, LANES), BF16),
        jax.ShapeDtypeStruct((nb, MLA_HEADS, ns, LANES), BF16),
        jax.ShapeDtypeStruct((nb, MLA_HEADS, ns, MLA_V), BF16),
        jax.ShapeDtypeStruct((n, MLA_ROW), F32),
        jax.ShapeDtypeStruct((nb, NSA_HEADS, ns, LANES), BF16),
        jax.ShapeDtypeStruct((n, width), F32),
        jax.ShapeDtypeStruct((n, width), F32),
        jax.ShapeDtypeStruct((n, width), F32),
        jax.ShapeDtypeStruct((nb, NSA_GROUPS, ns, LANES), BF16),
        jax.ShapeDtypeStruct((nb, NSA_GROUPS, ns, LANES), BF16),
        jax.ShapeDtypeStruct((n // CMP_BLOCK, width), F32),
        jax.ShapeDtypeStruct((n, LANES), F32),
        jax.ShapeDtypeStruct((n, d_model), F32),
        jax.ShapeDtypeStruct((n, d_model), F32),
    )
    out_specs = (
        head(MLA_HEADS, LANES), head(MLA_HEADS, LANES), head(MLA_HEADS, MLA_V), row(MLA_ROW),
        head(NSA_HEADS, LANES), row(width), row(width), row(width),
        head(NSA_GROUPS, LANES), head(NSA_GROUPS, LANES),
        pl.BlockSpec((tm // CMP_BLOCK, width), lambda i: (i, 0)),
        row(LANES), row(d_model), row(d_model),
    )
    in_specs = [row(d_model), _resident(ln.shape), _resident(w_all.shape), _resident(w_up.shape),
                _resident(kv_norm.shape), tab, tab, tab, tab, _resident(w_pool.shape)]
    return pl.pallas_call(
        _proj_kernel, grid=(n // tm,), in_specs=in_specs, out_specs=out_specs, out_shape=out_shape,
        compiler_params=_params(("arbitrary",)), name="proj",
    )(x, ln, w_all, w_up, kv_norm, *tabs, w_pool)


def _online_update(carry, s, v):
    m, l, acc = carry
    m_new = jnp.maximum(m, jnp.max(s, axis=-1, keepdims=True))
    alpha = jnp.exp(m - m_new)
    p = jnp.exp(s - m_new)
    l = alpha * l + jnp.sum(p, axis=-1, keepdims=True)
    acc = alpha * acc + _dot(p.astype(BF16), v)
    return m_new, l, acc


def _flash_init(rows, dv):
    return (jnp.full((rows, 1), NEG_INF, F32), jnp.zeros((rows, 1), F32), jnp.zeros((rows, dv), F32))


def _mla_flash_kernel(q_ref, k_ref, v_ref, o_ref, *, t):
    qi = pl.program_id(2)
    heads = q_ref.shape[1]
    rows = lax.broadcasted_iota(jnp.int32, (t, t), 0)
    cols = lax.broadcasted_iota(jnp.int32, (t, t), 1)
    outs = []
    for hh in range(heads):
        q = q_ref[0, hh]

        def tile(j, hh=hh):
            sl = pl.ds(pl.multiple_of(j * t, t), t)
            return k_ref[0, hh, sl, :], v_ref[0, hh, sl, :]

        def full_step(j, carry, q=q, tile=tile):
            k, v = tile(j)
            return _online_update(carry, _dot_nt(q, k), v)

        carry = lax.fori_loop(0, qi, full_step, _flash_init(t, v_ref.shape[3]))
        k, v = tile(qi)
        s = jnp.where(cols <= rows, _dot_nt(q, k), NEG_INF)
        _, l, acc = _online_update(carry, s, v)
        outs.append(acc / l)
    o_ref[0] = jnp.concatenate(outs, axis=1).astype(o_ref.dtype)


def _mla_flash(q, k, v, t):
    nb, nh, ns, _ = q.shape
    hp = LANES // MLA_V
    kern = functools.partial(_mla_flash_kernel, t=t)
    return pl.pallas_call(
        kern, grid=(nb, nh // hp, ns // t),
        in_specs=[pl.BlockSpec((1, hp, t, LANES), lambda b, h, i: (b, h, i, 0)),
                  pl.BlockSpec((1, hp, ns, LANES), lambda b, h, i: (b, h, 0, 0)),
                  pl.BlockSpec((1, hp, ns, MLA_V), lambda b, h, i: (b, h, 0, 0))],
        out_specs=pl.BlockSpec((1, t, LANES), lambda b, h, i: (b, i, h)),
        out_shape=jax.ShapeDtypeStruct((nb, ns, nh * MLA_V), BF16),
        compiler_params=_params(("arbitrary", "arbitrary", "arbitrary")), name="mla_flash",
    )(q, k, v)


def _masked_softmax(s, mask):
    s = jnp.where(mask, s, NEG_INF)
    m = jnp.max(s, axis=-1, keepdims=True)
    e = jnp.where(mask, jnp.exp(s - m), 0.0)
    return e / jnp.maximum(jnp.sum(e, axis=-1, keepdims=True), 1e-30)


def _top_blocks(imp, rounds):
    lane = lax.broadcasted_iota(jnp.int32, imp.shape, 1)
    sel = jnp.zeros(imp.shape, jnp.bool_)
    v = imp
    for _ in range(rounds):
        m = jnp.max(v, axis=-1, keepdims=True)
        first = jnp.min(jnp.where(v == m, lane, imp.shape[1]), axis=-1, keepdims=True)
        hit = lane == first
        sel = sel | hit
        v = jnp.where(hit, -jnp.inf, v)
    return sel


def _sum_heads(p, rep, rows):
    tot = p[0:rows]
    for r in range(1, rep):
        tot = tot + p[r * rows:(r + 1) * rows]
    return tot


def _tile_heads(x, rep):
    return jnp.concatenate([x] * rep, axis=0)


def _nsa_prompt_kernel(q_ref, slc_ref, win_ref, pool_ref, gate_ref, expand_ref, o_ref, *, tq, tk):
    qi = pl.program_id(1)
    nblk = pool_ref.shape[2]
    start = qi * tq
    qpos = start + lax.broadcasted_iota(jnp.int32, (tq, 1), 0)
    blk = lax.broadcasted_iota(jnp.int32, (tq, nblk), 1)
    cur = qpos // CMP_BLOCK
    rep = NSA_REP
    qpos4 = _tile_heads(qpos, rep)
    blk4 = lax.broadcasted_iota(jnp.int32, (rep * tq, nblk), 1)
    gates = gate_ref[0]
    outs = []
    for g in range(NSA_GROUPS):
        q = q_ref[0, g * rep:(g + 1) * rep].reshape(rep * tq, LANES)
        pool = pool_ref[0, g].astype(BF16)
        p_c = _masked_softmax(_dot_nt(q, pool), (blk4 + 1) * CMP_BLOCK - 1 <= qpos4)
        o_c = _dot(p_c.astype(BF16), pool)
        imp = _sum_heads(p_c, rep, tq)
        forced = (blk == 0) | (blk == cur) | (blk == cur - 1)
        imp = jnp.where(forced, jnp.inf, imp)
        imp = jnp.where(blk <= cur, imp, -jnp.inf)
        sel = _top_blocks(imp, SEL_TOPK) & (blk <= cur)
        sel_b = sel.astype(BF16)

        def sel_scores(j, q=q, sel_b=sel_b, g=g):
            sl = pl.ds(pl.multiple_of(j * tk, tk), tk)
            kv = slc_ref[0, g, sl, :]
            hit = _dot(sel_b, expand_ref[:, sl])
            bias = (hit - 1.0) * (-NEG_INF)
            return _dot_nt(q, kv) + _tile_heads(bias, rep), kv

        def sel_step(j, carry, sel_scores=sel_scores):
            s, kv = sel_scores(j)
            return _online_update(carry, s, kv)

        jd = start // tk
        carry = lax.fori_loop(0, jd, sel_step, _flash_init(rep * tq, LANES))
        s, kv = sel_scores(jd)
        kpos = jd * tk + lax.broadcasted_iota(jnp.int32, (1, tk), 1)
        s = jnp.where(kpos <= qpos4, s, NEG_INF)
        _, l, acc = _online_update(carry, s, kv)
        o_s = acc / l
        ntile = WINDOW // tq + 1
        scores, tiles = [], []
        for c in range(ntile):
            idx = qi - (ntile - 1) + c
            sl = pl.ds(pl.multiple_of(jnp.maximum(idx, 0) * tq, tq), tq)
            kvw = win_ref[0, g, sl, :]
            kp = idx * tq + lax.broadcasted_iota(jnp.int32, (1, tq), 1)
            ok = (kp <= qpos4) & (kp > qpos4 - WINDOW) & (kp >= 0)
            scores.append(jnp.where(ok, _dot_nt(q, kvw), NEG_INF))
            tiles.append(kvw)
        s_w = jnp.concatenate(scores, axis=1)
        m = jnp.max(s_w, axis=-1, keepdims=True)
        e = jnp.exp(s_w - m)
        p_w = (e / jnp.sum(e, axis=-1, keepdims=True)).astype(BF16)
        o_w = _dot(p_w[:, 0:tq], tiles[0])
        for c in range(1, ntile):
            o_w = o_w + _dot(p_w[:, c * tq:(c + 1) * tq], tiles[c])
        for r in range(rep):
            hh = g * rep + r
            rs = slice(r * tq, (r + 1) * tq)
            y = (gates[:, 3 * hh:3 * hh + 1] * o_c[rs] + gates[:, 3 * hh + 1:3 * hh + 2] * o_s[rs]
                 + gates[:, 3 * hh + 2:3 * hh + 3] * o_w[rs])
            outs.append(y[:, NSA_DIM:])
    o_ref[0] = jnp.concatenate(outs, axis=1).astype(o_ref.dtype)


def _nsa_prompt(q, slc, win, pool, gates, expand, tq, tk):
    nb, nh, ns, _ = q.shape
    nblk = ns // CMP_BLOCK
    kern = functools.partial(_nsa_prompt_kernel, tq=tq, tk=tk)
    whole = lambda c: pl.BlockSpec((1, c, ns, LANES), lambda b, i: (b, 0, 0, 0))
    return pl.pallas_call(
        kern, grid=(nb, ns // tq),
        in_specs=[pl.BlockSpec((1, nh, tq, LANES), lambda b, i: (b, 0, i, 0)),
                  whole(NSA_GROUPS), whole(NSA_GROUPS),
                  pl.BlockSpec((1, NSA_GROUPS, nblk, LANES), lambda b, i: (b, 0, 0, 0)),
                  pl.BlockSpec((1, tq, LANES), lambda b, i: (b, i, 0)),
                  _resident(expand.shape)],
        out_specs=pl.BlockSpec((1, tq, nh * NSA_DIM), lambda b, i: (b, i, 0)),
        out_shape=jax.ShapeDtypeStruct((nb, ns, nh * NSA_DIM), BF16),
        compiler_params=_params(("arbitrary", "arbitrary")), name="nsa_prompt",
    )(q, slc, win, pool, gates, expand)


def _head_matmul_kernel(a_ref, b_ref, o_ref):
    o_ref[0] = _dot(a_ref[0], b_ref[0]).astype(o_ref.dtype)


def _head_matmul(a, b, dtype):
    nh, m, k = a.shape
    n = b.shape[2]
    return pl.pallas_call(
        _head_matmul_kernel, grid=(nh,),
        in_specs=[pl.BlockSpec((1, m, k), lambda h: (h, 0, 0)), pl.BlockSpec((1, k, n), lambda h: (h, 0, 0))],
        out_specs=pl.BlockSpec((1, m, n), lambda h: (h, 0, 0)),
        out_shape=jax.ShapeDtypeStruct((nh, m, n), dtype),
        compiler_params=_params(("arbitrary",)), name="head_matmul",
    )(a, b)


def _new_token_mask(rows, t_new):
    tok = lax.broadcasted_iota(jnp.int32, (rows, LANES), 0) % t_new
    lane = lax.broadcasted_iota(jnp.int32, (rows, LANES), 1)
    return (lane < t_new) & (lane <= tok)


def _mla_sample_kernel(pt_ref, q_ref, new_ref, *rest, pages, t_new):
    page_refs, (o_ref, kv_scr, new_scr, m_scr, l_scr, acc_scr) = rest[:pages], rest[pages:]
    j = pl.program_id(1)
    width = kv_scr.shape[1]

    @pl.when((pl.program_id(0) == 0) & (j == 0))
    def _():
        kv_scr[...] = jnp.zeros(kv_scr.shape, BF16)

    @pl.when(j == 0)
    def _():
        m0, l0, a0 = _flash_init(q_ref.shape[1], width)
        m_scr[...], l_scr[...], acc_scr[...] = m0, l0, a0

    for p in range(pages):
        kv_scr[p * PAGE_SIZE:(p + 1) * PAGE_SIZE, 0:MLA_ROW] = page_refs[p][0].astype(BF16)
    q = q_ref[0]
    kv = kv_scr[...]
    carry = _online_update((m_scr[...], l_scr[...], acc_scr[...]), _dot_nt(q, kv), kv)
    m_scr[...], l_scr[...], acc_scr[...] = carry

    @pl.when(j == pl.num_programs(1) - 1)
    def _():
        new_scr[...] = jnp.zeros(new_scr.shape, BF16)
        new_scr[0:t_new, 0:MLA_ROW] = new_ref[0].astype(BF16)
        kvn = new_scr[...]
        s = jnp.where(_new_token_mask(q.shape[0], t_new), _dot_nt(q, kvn), NEG_INF)
        _, l, acc = _online_update(carry, s, kvn)
        o_ref[0] = acc / l


def _page_specs(pages, width):
    return [pl.BlockSpec((1, PAGE_SIZE, width), lambda b, j, pt, p=p: (pt[b, j * pages + p], 0, 0))
            for p in range(pages)]


def _mla_sample(page_table, q, new_rows, cache, pages):
    nb, rows, width = q.shape
    t_new = new_rows.shape[1]
    n_pages = page_table.shape[1]
    kern = functools.partial(_mla_sample_kernel, pages=pages, t_new=t_new)
    grid_spec = pltpu.PrefetchScalarGridSpec(
        num_scalar_prefetch=1, grid=(nb, n_pages // pages),
        in_specs=[pl.BlockSpec((1, rows, width), lambda b, j, pt: (b, 0, 0)),
                  pl.BlockSpec((1, t_new, MLA_ROW), lambda b, j, pt: (b, 0, 0))] + _page_specs(pages, MLA_ROW),
        out_specs=pl.BlockSpec((1, rows, width), lambda b, j, pt: (b, 0, 0)),
        scratch_shapes=[pltpu.VMEM((pages * PAGE_SIZE, width), BF16), pltpu.VMEM((LANES, width), BF16),
                        pltpu.VMEM((rows, 1), F32), pltpu.VMEM((rows, 1), F32), pltpu.VMEM((rows, width), F32)])
    return pl.pallas_call(
        kern, grid_spec=grid_spec, out_shape=jax.ShapeDtypeStruct((nb, rows, width), F32),
        compiler_params=_params(("arbitrary", "arbitrary")), name="mla_sample",
    )(page_table, q, new_rows, *([cache] * pages))


def _cmp_pool_kernel(pt_ref, w_ref, *rest, pages):
    page_refs, o_ref = rest[:pages], rest[pages]
    per_page = PAGE_SIZE // CMP_BLOCK
    w = w_ref[...][None]
    for p in range(pages):
        rows = page_refs[p][0]
        o_ref[0, p * per_page:(p + 1) * per_page, :] = jnp.sum(
            rows.reshape(per_page, CMP_BLOCK, rows.shape[1]) * w, axis=1)


def _cmp_pool(page_table, w_pool, cache, pages):
    nb, n_pages = page_table.shape
    width = cache.shape[2]
    per_page = PAGE_SIZE // CMP_BLOCK
    kern = functools.partial(_cmp_pool_kernel, pages=pages)
    grid_spec = pltpu.PrefetchScalarGridSpec(
        num_scalar_prefetch=1, grid=(nb, n_pages // pages),
        in_specs=[pl.BlockSpec(w_pool.shape, lambda b, j, pt: (0, 0))] + _page_specs(pages, width),
        out_specs=pl.BlockSpec((1, pages * per_page, width), lambda b, j, pt: (b, j, 0)))
    return pl.pallas_call(
        kern, grid_spec=grid_spec, out_shape=jax.ShapeDtypeStruct((nb, n_pages * per_page, width), F32),
        compiler_params=_params(("arbitrary", "arbitrary")), name="cmp_pool",
    )(page_table, w_pool, *([cache] * pages))


def _pack_group(rows, g):
    ko, vo = g * NSA_DIM, NSA_GROUPS * NSA_DIM + g * NSA_DIM
    return jnp.concatenate([rows[:, ko:ko + NSA_DIM], rows[:, vo:vo + NSA_DIM]], axis=1).astype(BF16)


def _nsa_sample_kernel(pt_ref, q_ref, pool_ref, gate_ref, win_ref, wnew_ref, snew_ref, expand_ref, *rest,
                       pages, t_new):
    page_refs = rest[:pages]
    o_ref, kv_scr, new_scr, bias_scr, oc_scr, ow_scr, m_scr, l_scr, acc_scr = rest[pages:]
    j = pl.program_id(1)
    rep = NSA_REP
    rows = rep * t_new
    nblk = pool_ref.shape[1]
    tk = pages * PAGE_SIZE
    new_ok = _new_token_mask(rows, t_new)

    def new_tile(ref, g):
        new_scr[...] = jnp.zeros(new_scr.shape, BF16)
        new_scr[0:t_new, :] = _pack_group(ref[0], g)
        return new_scr[...]

    @pl.when(j == 0)
    def _():
        pool = pool_ref[0]
        win = win_ref[0]
        wb = win.shape[0]
        tok = lax.broadcasted_iota(jnp.int32, (rows, 1), 0) % t_new
        for g in range(NSA_GROUPS):
            q = q_ref[0, g * rows:(g + 1) * rows]
            pg = _pack_group(pool, g)
            s_c = _dot_nt(q, pg)
            e_c = jnp.exp(s_c - jnp.max(s_c, axis=-1, keepdims=True))
            p_c = e_c / jnp.sum(e_c, axis=-1, keepdims=True)
            oc_scr[g] = _dot(p_c.astype(BF16), pg)
            tot = p_c
            for r in range(1, rep):
                tot = tot + pltpu.roll(p_c, rows - r * t_new, 0)
            tot = jnp.where(lax.broadcasted_iota(jnp.int32, (rows, nblk), 0) < t_new, tot, 0.0)
            imp = tot
            for r in range(1, rep):
                imp = imp + pltpu.roll(tot, r * t_new, 0)
            blk = lax.broadcasted_iota(jnp.int32, (rows, nblk), 1)
            imp = jnp.where((blk == 0) | (blk == nblk - 1), jnp.inf, imp)
            bias_scr[g] = jnp.where(_top_blocks(imp, SEL_TOPK - 1), 1.0, 0.0)
            wg = _pack_group(win, g)
            i = lax.broadcasted_iota(jnp.int32, (1, wb), 1)
            ok_w = i > tok + (wb - WINDOW)
            ng = new_tile(wnew_ref, g)
            s_w = jnp.concatenate([jnp.where(ok_w, _dot_nt(q, wg), NEG_INF),
                                   jnp.where(new_ok, _dot_nt(q, ng), NEG_INF)], axis=1)
            m = jnp.max(s_w, axis=-1, keepdims=True)
            e = jnp.exp(s_w - m)
            p_w = (e / jnp.sum(e, axis=-1, keepdims=True)).astype(BF16)
            ow_scr[g] = _dot(p_w[:, :wb], wg) + _dot(p_w[:, wb:], ng)
            m0, l0, a0 = _flash_init(rows, LANES)
            m_scr[g], l_scr[g], acc_scr[g] = m0, l0, a0

    for p in range(pages):
        page = page_refs[p][0]
        for g in range(NSA_GROUPS):
            kv_scr[g, p * PAGE_SIZE:(p + 1) * PAGE_SIZE, :] = _pack_group(page, g)
    sl = pl.ds(pl.multiple_of(j * tk, tk), tk)
    for g in range(NSA_GROUPS):
        q = q_ref[0, g * rows:(g + 1) * rows]
        kv = kv_scr[g]
        hit = _dot(bias_scr[g].astype(BF16), expand_ref[:, sl])
        s = _dot_nt(q, kv) + (hit - 1.0) * (-NEG_INF)
        carry = _online_update((m_scr[g], l_scr[g], acc_scr[g]), s, kv)
        m_scr[g], l_scr[g], acc_scr[g] = carry

    @pl.when(j == pl.num_programs(1) - 1)
    def _():
        gates = gate_ref[0]
        for g in range(NSA_GROUPS):
            q = q_ref[0, g * rows:(g + 1) * rows]
            ng = new_tile(snew_ref, g)
            s = jnp.where(new_ok, _dot_nt(q, ng), NEG_INF)
            _, l, acc = _online_update((m_scr[g], l_scr[g], acc_scr[g]), s, ng)
            gg = gates[g * rows:(g + 1) * rows]
            o_ref[0, g * rows:(g + 1) * rows, :] = (gg[:, 0:1] * oc_scr[g] + gg[:, 1:2] * (acc / l)
                                                    + gg[:, 2:3] * ow_scr[g])


def _nsa_sample(page_table, q, pool, gates, win, win_new, slc_new, expand, cache, pages):
    nb, rows2, _ = q.shape
    t_new = slc_new.shape[1]
    n_pages = page_table.shape[1]
    width = cache.shape[2]
    rows = rows2 // NSA_GROUPS
    per_b = lambda shape: pl.BlockSpec((1,) + shape, lambda b, j, pt: (b, 0, 0))
    kern = functools.partial(_nsa_sample_kernel, pages=pages, t_new=t_new)
    grid_spec = pltpu.PrefetchScalarGridSpec(
        num_scalar_prefetch=1, grid=(nb, n_pages // pages),
        in_specs=[per_b((rows2, LANES)), per_b(pool.shape[1:]), per_b((rows2, LANES)), per_b(win.shape[1:]),
                  per_b((t_new, width)), per_b((t_new, width)),
                  pl.BlockSpec(expand.shape, lambda b, j, pt: (0, 0))] + _page_specs(pages, width),
        out_specs=per_b((rows2, LANES)),
        scratch_shapes=[pltpu.VMEM((NSA_GROUPS, pages * PAGE_SIZE, LANES), BF16), pltpu.VMEM((LANES, LANES), BF16),
                        pltpu.VMEM((NSA_GROUPS, rows, expand.shape[0]), F32),
                        pltpu.VMEM((NSA_GROUPS, rows, LANES), F32), pltpu.VMEM((NSA_GROUPS, rows, LANES), F32),
                        pltpu.VMEM((NSA_GROUPS, rows, 1), F32), pltpu.VMEM((NSA_GROUPS, rows, 1), F32),
                        pltpu.VMEM((NSA_GROUPS, rows, LANES), F32)])
    return pl.pallas_call(
        kern, grid_spec=grid_spec, out_shape=jax.ShapeDtypeStruct((nb, rows2, LANES), F32),
        compiler_params=_params(("arbitrary", "arbitrary")), name="nsa_sample",
    )(page_table, q, pool, gates, win, win_new, slc_new, expand, *([cache] * pages))


def _tail_kernel(x_ref, ya_ref, yb_ref, ga_ref, gb_ref, p_ref, wba_ref, wbn_ref, wo_ref, lnf_ref,
                 wg_ref, wu_ref, wd_ref, lnp_ref, wpg_ref, wpp_ref, lnl_ref, o_ref, *, chunk):
    merged = ga_ref[...] * _dot(ya_ref[...], wba_ref[...]) + gb_ref[...] * _dot(yb_ref[...], wbn_ref[...])
    x = x_ref[...] + _dot(merged.astype(BF16), wo_ref[...])
    h = _rms(x, lnf_ref[...]).astype(BF16)

    def ffn(c, acc):
        sl = pl.ds(pl.multiple_of(c * chunk, chunk), chunk)
        hidden = jax.nn.silu(_dot(h, wg_ref[:, sl])) * _dot(h, wu_ref[:, sl])
        return acc + _dot(hidden.astype(BF16), wd_ref[sl, :])

    x = x + lax.fori_loop(0, wg_ref.shape[1] // chunk, ffn, jnp.zeros(x.shape, F32))
    gate = jax.nn.sigmoid(_dot(_rms(x, lnp_ref[...]).astype(BF16), wpg_ref[...]))
    x = x + gate * _dot(p_ref[...].astype(BF16), wpp_ref[...])
    o_ref[...] = _rms(x, lnl_ref[...])


def _tail(x, ya, yb, ga, gb, p, weights, tm):
    n, d_model = x.shape
    row = lambda a: pl.BlockSpec((tm, a.shape[1]), lambda i: (i, 0))
    acts = (x, ya, yb, ga, gb, p)
    kern = functools.partial(_tail_kernel, chunk=2 * LANES)
    return pl.pallas_call(
        kern, grid=(n // tm,),
        in_specs=[row(a) for a in acts] + [_resident(w.shape) for w in weights],
        out_specs=pl.BlockSpec((tm, d_model), lambda i: (i, 0)),
        out_shape=jax.ShapeDtypeStruct((n, d_model), F32),
        compiler_params=_params(("arbitrary",)), name="tail",
    )(*acts, *weights)


def _rope_tables(pos):
    pos = pos.astype(F32)[:, None]
    n = pos.shape[0]

    def cs(rot):
        half = rot // 2
        inv = jnp.float32(ROPE_THETA) ** (-jnp.arange(half, dtype=F32) * 2.0 / rot)
        ang = pos * inv[None, :]
        return jnp.cos(ang), jnp.sin(ang)

    one = lambda w: jnp.ones((n, w), F32)
    zero = lambda w: jnp.zeros((n, w), F32)
    c, s = cs(MLA_ROPE)
    pad = LANES - MLA_NOPE - MLA_ROPE
    cm = jnp.concatenate([one(MLA_NOPE), c, c, one(pad)], axis=1)
    sm = jnp.concatenate([zero(MLA_NOPE), -s, s, zero(pad)], axis=1)
    c, s = cs(NSA_ROT)
    rest = NSA_DIM - NSA_ROT
    cn = jnp.concatenate([c, c, one(rest)] * (LANES // NSA_DIM), axis=1)
    sn = jnp.concatenate([-s, s, zero(rest)] * (LANES // NSA_DIM), axis=1)
    return cm, sm, cn, sn


def _layout_w_in(w_in):
    d_model = w_in.shape[0]
    sizes = (MLA_HEADS * (MLA_NOPE + MLA_ROPE), MLA_KV_LORA, MLA_ROPE, NSA_HEADS * NSA_DIM,
             SEG_KV // 3, SEG_KV // 3, SEG_KV // 3, 3 * NSA_HEADS, 2 * d_model)
    offs = np.concatenate([[0], np.cumsum(sizes)])
    qa, ca, ra, qb, zc, zs, zw, gn, gm = [w_in[:, int(offs[i]):int(offs[i + 1])] for i in range(len(sizes))]
    qa = qa.reshape(d_model, MLA_HEADS, MLA_NOPE + MLA_ROPE)
    qa = jnp.pad(qa, ((0, 0), (0, 0), (0, LANES - MLA_NOPE - MLA_ROPE))).reshape(d_model, SEG_QA)
    ra = jnp.pad(ra, ((0, 0), (MLA_NOPE, LANES - MLA_NOPE - MLA_ROPE)))
    gn = jnp.pad(gn, ((0, 0), (0, SEG_GN - gn.shape[1])))
    return jnp.concatenate([qa, ca, ra, qb, zc, zs, zw, gn, gm], axis=1).astype(BF16)


def _layout_w_up(w_uk, w_uv):
    wk = jnp.pad(w_uk, ((0, 0), (0, 0), (0, LANES - MLA_NOPE))).reshape(MLA_KV_LORA, SEG_QA)
    return jnp.concatenate([wk, w_uv.reshape(MLA_KV_LORA, MLA_HEADS * MLA_V)], axis=1).astype(BF16)


def _layout_absorb(w_uk, w_uv, width):
    to_lat = jnp.zeros((MLA_HEADS, LANES, width), F32)
    to_lat = to_lat.at[:, 0:MLA_NOPE, 0:MLA_KV_LORA].set(w_uk.transpose(1, 2, 0))
    to_lat = to_lat.at[:, MLA_NOPE:MLA_NOPE + MLA_ROPE, MLA_KV_LORA:MLA_ROW].set(jnp.eye(MLA_ROPE, dtype=F32))
    to_val = jnp.zeros((MLA_HEADS, width, MLA_V), F32).at[:, 0:MLA_KV_LORA, :].set(w_uv.transpose(1, 0, 2))
    return to_lat.astype(BF16), to_val.astype(BF16)


def _block_expand(nblk, nkeys):
    return (jnp.arange(nkeys)[None, :] // CMP_BLOCK == jnp.arange(nblk)[:, None]).astype(BF16)


def kernel(x_prompt, x_sample, cache_mla, cache_nsa_cmp, cache_nsa_slc, state_nsa_win, page_table, p_prompt, p_sample, ln_attn, w_in, mla_kv_norm, mla_w_uk, mla_w_uv, nsa_w_cmp_k, nsa_w_cmp_v, w_branch_mla, w_branch_nsa, w_out, ln_ffn, w_ffn_gate, w_ffn_up, w_ffn_down, ln_ple, w_ple_gate, w_ple_proj, ln_final):
    nb, ns, d_model = x_prompt.shape
    db, t_new, _ = x_sample.shape
    depth = w_in.shape[0]
    n_pages = page_table.shape[1]
    past = n_pages * PAGE_SIZE
    wb = state_nsa_win.shape[2]
    width = 2 * NSA_GROUPS * NSA_DIM
    g_, d_ = NSA_GROUPS, NSA_DIM
    assert depth == 1 and t_new <= CMP_BLOCK and past % CMP_BLOCK == 0 and wb == WINDOW and past >= wb
    assert ns % 512 == 0 and (db * t_new) % 512 == 0
    tm = 512
    pages = min(16, n_pages)
    lat_w = 3 * LANES
    i = 0

    w_all = _layout_w_in(w_in[i])
    w_up = _layout_w_up(mla_w_uk[i], mla_w_uv[i])
    to_lat, to_val = _layout_absorb(mla_w_uk[i], mla_w_uv[i], lat_w)
    w_pool = jnp.concatenate([nsa_w_cmp_k[i]] * g_ + [nsa_w_cmp_v[i]] * g_, axis=1)
    ln_a = ln_attn[i][None, :]
    kvn = mla_kv_norm[i][None, :]
    tail_w = (w_branch_mla[i].astype(BF16), w_branch_nsa[i].astype(BF16), w_out[i].astype(BF16),
              ln_ffn[i][None, :], w_ffn_gate[i].astype(BF16), w_ffn_up[i].astype(BF16),
              w_ffn_down[i].astype(BF16), ln_ple[i][None, :], w_ple_gate[i].astype(BF16),
              w_ple_proj[i].astype(BF16), ln_final[None, :])

    xp = x_prompt.reshape(nb * ns, d_model)
    (qm, km, vm, mla_p, qn, kvc_p, kvs_p, kvw_p, slc, win, pool, gn, ga, gb) = _proj(
        xp, _rope_tables(jnp.arange(ns)), ln_a, w_all, w_up, kvn, w_pool, nb, ns, tm)
    ya = _mla_flash(qm, km, vm, 512)
    nblk = ns // CMP_BLOCK
    pool_g = jnp.concatenate([pool.reshape(nb, nblk, 2, g_, d_)[:, :, 0].transpose(0, 2, 1, 3),
                              pool.reshape(nb, nblk, 2, g_, d_)[:, :, 1].transpose(0, 2, 1, 3)], axis=-1)
    yb = _nsa_prompt(qn, slc, win, pool_g, gn.reshape(nb, ns, LANES), _block_expand(nblk, ns), 128, 512)
    y_prompt = _tail(xp, ya.reshape(nb * ns, -1), yb.reshape(nb * ns, -1), ga, gb,
                     p_prompt[i].reshape(nb * ns, -1), tail_w, tm).reshape(nb, ns, d_model)

    n_s = db * t_new
    xs = x_sample.reshape(n_s, d_model)
    pos_s = jnp.tile(past + jnp.arange(t_new), db)
    (qm, _, _, mla_s, qn, kvc_s, kvs_s, kvw_s, _, _, _, gn, ga, gb) = _proj(
        xs, _rope_tables(pos_s), ln_a, w_all, w_up, kvn, w_pool, 1, n_s, tm)
    by_batch = lambda a: a.reshape(a.shape[0], db, t_new, a.shape[-1]).transpose(1, 0, 2, 3).reshape(
        db, a.shape[0] * t_new, a.shape[-1])
    q_lat = by_batch(_head_matmul(qm[0], to_lat, BF16))
    o_lat = _mla_sample(page_table, q_lat, mla_s.reshape(db, t_new, MLA_ROW), cache_mla[i], pages)
    o_lat = o_lat.reshape(db, MLA_HEADS, t_new, lat_w).transpose(1, 0, 2, 3).reshape(MLA_HEADS, n_s, lat_w)
    ya = _head_matmul(o_lat.astype(BF16), to_val, BF16).transpose(1, 0, 2).reshape(n_s, MLA_HEADS * MLA_V)

    cmp_cache = cache_nsa_cmp[i].reshape(-1, PAGE_SIZE, width)
    slc_cache = cache_nsa_slc[i].reshape(-1, PAGE_SIZE, width)
    pool_s = _cmp_pool(page_table, w_pool, cmp_cache, pages)
    gate_s = gn[:, :3 * NSA_HEADS].reshape(db, t_new, NSA_HEADS, 3).transpose(0, 2, 1, 3).reshape(
        db, NSA_HEADS * t_new, 3)
    gate_s = jnp.pad(gate_s, ((0, 0), (0, 0), (0, LANES - 3)))
    win_state = state_nsa_win[i].reshape(db, wb, width)
    o_nsa = _nsa_sample(page_table, by_batch(qn[0]), pool_s, gate_s, win_state,
                        kvw_s.reshape(db, t_new, width), kvs_s.reshape(db, t_new, width),
                        _block_expand(past // CMP_BLOCK, past), slc_cache, pages)
    yb = o_nsa[:, :, NSA_DIM:].reshape(db, NSA_HEADS, t_new, d_).transpose(0, 2, 1, 3).reshape(n_s, -1)
    y_sample = _tail(xs, ya, yb.astype(BF16), ga, gb, p_sample[i].reshape(n_s, -1), tail_w, tm).reshape(
        db, t_new, d_model)

    kv5 = lambda a, b, s: a.reshape(1, b, s, 2, g_, d_)
    win_all = jnp.concatenate([win_state, kvw_s.reshape(db, t_new, width)], axis=1)
    wkeep = min(WINDOW, ns)
    return (y_prompt, y_sample,
            mla_p.reshape(1, nb, ns, MLA_ROW), mla_s.reshape(1, db, t_new, MLA_ROW),
            kv5(kvc_p, nb, ns), kv5(kvc_s, db, t_new), kv5(kvs_p, nb, ns), kv5(kvs_s, db, t_new),
            kv5(kvw_p, nb, ns)[:, :, ns - wkeep:],
            kv5(win_all[:, -min(WINDOW, past + t_new):], db, min(WINDOW, past + t_new)))
```

```python
import functools
import math

import jax
import jax.numpy as jnp
import numpy as np
from jax import lax
from jax.experimental import pallas as pl
from jax.experimental.pallas import tpu as pltpu

F32 = jnp.float32
BF16 = jnp.bfloat16

NORM_EPS = 1e-6
ROPE_THETA = 500000.0
NEG_INF = -1e30
LOG2E = math.log2(math.e)
PAGE_SIZE = 128
MLA_HEADS = 8
MLA_NOPE = 64
MLA_ROPE = 32
MLA_V = 64
MLA_KV_LORA = 256
MLA_ROW = MLA_KV_LORA + MLA_ROPE
NSA_HEADS = 8
NSA_GROUPS = 2
NSA_REP = NSA_HEADS // NSA_GROUPS
NSA_DIM = 64
NSA_ROT = NSA_DIM // 4
CMP_BLOCK = 64
SEL_TOPK = 16
WINDOW = 512
LANES = 128
VMEM_LIMIT = 60 * 1024 * 1024

SEG_QA = MLA_HEADS * LANES
SEG_C = MLA_KV_LORA
SEG_R = LANES
SEG_QB = NSA_HEADS * NSA_DIM
SEG_KV = 3 * 2 * NSA_GROUPS * NSA_DIM
SEG_GN = LANES
OFF_C = SEG_QA
OFF_R = OFF_C + SEG_C
OFF_QB = OFF_R + SEG_R
OFF_KV = OFF_QB + SEG_QB
OFF_GN = OFF_KV + SEG_KV
OFF_GM = OFF_GN + SEG_GN


def _params(sem):
    return pltpu.CompilerParams(dimension_semantics=sem, vmem_limit_bytes=VMEM_LIMIT)


def _resident(shape):
    nd = len(shape)
    return pl.BlockSpec(shape, lambda *_: (0,) * nd, pipeline_mode=pl.Buffered(1))


def _dot(a, b):
    return jnp.dot(a, b, preferred_element_type=F32)


def _dot_nt(a, b):
    return lax.dot_general(a, b, (((1,), (1,)), ((), ())), preferred_element_type=F32)


def _rms(x, g):
    return x * lax.rsqrt(jnp.mean(x * x, axis=-1, keepdims=True) + NORM_EPS) * g


def _value_slot(v, rows):
    lane = lax.broadcasted_iota(jnp.int32, (rows, LANES - v.shape[1]), 1)
    return jnp.concatenate([v, jnp.where(lane == 0, 1.0, 0.0)], axis=1).astype(BF16)


def _proj_kernel(x_ref, ln_ref, w_ref, wup_ref, kvn_ref, cm_ref, sm_ref, cn_ref, sn_ref, wpool_ref,
                 qm_ref, km_ref, vm_ref, mla_ref, qn_ref, kvc_ref, kvs_ref, kvw_ref,
                 ka_ref, va_ref, win_ref, pool_ref, gn_ref, ga_ref, gb_ref, *, nt):
    tm = x_ref.shape[0]
    d_model = x_ref.shape[1]
    nblk = ka_ref.shape[3] - LANES
    h = _rms(x_ref[...], ln_ref[...]).astype(BF16)
    lane = lax.broadcasted_iota(jnp.int32, (tm, LANES), 1)
    cm, sm, cn, sn = cm_ref[...], sm_ref[...], cn_ref[...], sn_ref[...]

    def rope_m(z):
        half = MLA_ROPE // 2
        sw = jnp.where(lane < MLA_NOPE + half, pltpu.roll(z, LANES - half, 1), pltpu.roll(z, half, 1))
        return z * cm + sw * sm

    def rope_n(z):
        half = NSA_ROT // 2
        sw = jnp.where((lane & (NSA_DIM - 1)) < half, pltpu.roll(z, LANES - half, 1), pltpu.roll(z, half, 1))
        return z * cn + sw * sn

    zq = _dot(h, w_ref[:, 0:SEG_QA])
    scale_m = (MLA_NOPE + MLA_ROPE) ** -0.5 * LOG2E
    for hh in range(MLA_HEADS):
        qm_ref[0, hh] = (rope_m(zq[:, hh * LANES:(hh + 1) * LANES]) * scale_m).astype(BF16)

    zc = _dot(h, w_ref[:, OFF_C:OFF_R])
    c = _rms(zc, kvn_ref[...])
    kpe = rope_m(_dot(h, w_ref[:, OFF_R:OFF_QB]))
    mla_ref[:, 0:MLA_KV_LORA] = c
    mla_ref[:, MLA_KV_LORA:MLA_ROW] = kpe[:, MLA_NOPE:MLA_NOPE + MLA_ROPE]
    up = _dot(c.astype(BF16), wup_ref[...])
    for hh in range(MLA_HEADS):
        km_ref[0, hh] = (up[:, hh * LANES:(hh + 1) * LANES] + kpe).astype(BF16)
        vo = SEG_QA + hh * MLA_V
        vm_ref[0, hh] = _value_slot(up[:, vo:vo + MLA_V], tm)

    zqb = _dot(h, w_ref[:, OFF_QB:OFF_KV])
    zero_half = jnp.zeros((tm, NSA_DIM), F32)
    scale_n = NSA_DIM ** -0.5 * LOG2E
    for pair in range(NSA_HEADS // 2):
        zz = rope_n(zqb[:, pair * LANES:(pair + 1) * LANES]) * scale_n
        qn_ref[0, 2 * pair] = jnp.concatenate([zz[:, :NSA_DIM], zero_half], axis=1).astype(BF16)
        qn_ref[0, 2 * pair + 1] = jnp.concatenate([zz[:, NSA_DIM:], zero_half], axis=1).astype(BF16)

    zkv = _dot(h, w_ref[:, OFF_KV:OFF_GN])
    width = 2 * NSA_GROUPS * NSA_DIM

    def kv_rows(i):
        z = zkv[:, i * width:(i + 1) * width]
        return jnp.concatenate([rope_n(z[:, :LANES]), z[:, LANES:]], axis=1)

    rows_c, rows_s, rows_w = kv_rows(0), kv_rows(1), kv_rows(2)
    kvc_ref[...] = rows_c
    kvs_ref[...] = rows_s
    kvw_ref[...] = rows_w
    pos = (pl.program_id(0) % nt) * tm + lax.broadcasted_iota(jnp.int32, (tm, nblk), 0)
    onehot = jnp.where(lax.broadcasted_iota(jnp.int32, (tm, nblk), 1) == pos // CMP_BLOCK, 1.0, 0.0)
    for g in range(NSA_GROUPS):
        ko, vo = g * NSA_DIM, LANES + g * NSA_DIM
        ka_ref[0, g] = jnp.concatenate([rows_s[:, ko:ko + NSA_DIM], zero_half, onehot], axis=1).astype(BF16)
        va_ref[0, g] = _value_slot(rows_s[:, vo:vo + NSA_DIM], tm)
        win_ref[0, g] = jnp.concatenate([rows_w[:, ko:ko + NSA_DIM], rows_w[:, vo:vo + NSA_DIM]],
                                        axis=1).astype(BF16)
    pool_ref[...] = jnp.sum(rows_c.reshape(tm // CMP_BLOCK, CMP_BLOCK, width) * wpool_ref[...][None], axis=1)

    gn_ref[...] = jax.nn.sigmoid(_dot(h, w_ref[:, OFF_GN:OFF_GM]))
    zm = _dot(h, w_ref[:, OFF_GM:OFF_GM + 2 * d_model])
    ga_ref[...] = jax.nn.sigmoid(zm[:, :d_model])
    gb_ref[...] = jax.nn.sigmoid(zm[:, d_model:])


def _proj(x, tabs, ln, w_all, w_up, kv_norm, w_pool, nb, ns, tm, nblk):
    n, d_model = x.shape
    nt = ns // tm
    width = 2 * NSA_GROUPS * NSA_DIM
    row = lambda w: pl.BlockSpec((tm, w), lambda i: (i, 0))
    head = lambda hn, w: pl.BlockSpec((1, hn, tm, w), lambda i: (i // nt, 0, i % nt, 0))
    tab = pl.BlockSpec((tm, LANES), lambda i: (i % nt, 0))
    slots = (
        (head(MLA_HEADS, LANES), (nb, MLA_HEADS, ns, LANES), BF16),
        (head(MLA_HEADS, LANES), (nb, MLA_HEADS, ns, LANES), BF16),
        (head(MLA_HEADS, LANES), (nb, MLA_HEADS, ns, LANES), BF16),
        (row(MLA_ROW), (n, MLA_ROW), F32),
        (head(NSA_HEADS, LANES), (nb, NSA_HEADS, ns, LANES), BF16),
        (row(width), (n, width), F32),
        (row(width), (n, width), F32),
        (row(width), (n, width), F32),
        (head(NSA_GROUPS, LANES + nblk), (nb, NSA_GROUPS, ns, LANES + nblk), BF16),
        (head(NSA_GROUPS, LANES), (nb, NSA_GROUPS, ns, LANES), BF16),
        (head(NSA_GROUPS, LANES), (nb, NSA_GROUPS, ns, LANES), BF16),
        (pl.BlockSpec((tm // CMP_BLOCK, width), lambda i: (i, 0)), (n // CMP_BLOCK, width), F32),
        (row(LANES), (n, LANES), F32),
        (row(d_model), (n, d_model), F32),
        (row(d_model), (n, d_model), F32),
    )
    in_specs = [row(d_model), _resident(ln.shape), _resident(w_all.shape), _resident(w_up.shape),
                _resident(kv_norm.shape), tab, tab, tab, tab, _resident(w_pool.shape)]
    return pl.pallas_call(
        functools.partial(_proj_kernel, nt=nt), grid=(n // tm,), in_specs=in_specs,
        out_specs=tuple(s[0] for s in slots),
        out_shape=tuple(jax.ShapeDtypeStruct(s[1], s[2]) for s in slots),
        compiler_params=_params(("arbitrary",)), name="proj",
    )(x, ln, w_all, w_up, kv_norm, *tabs, w_pool)


def _flash_update(m, acc, s, v):
    m_new = jnp.maximum(m, jnp.max(s, axis=-1, keepdims=True))
    alpha = jnp.exp2(m - m_new)
    p = jnp.exp2(s - m_new).astype(BF16)
    return m_new, alpha * acc + _dot(p, v)


def _flash_finish(acc, dv):
    return acc[:, :dv] / acc[:, dv:dv + 1]


def _online_update(carry, s, pv):
    m, l, acc = carry
    m_new = jnp.maximum(m, jnp.max(s, axis=-1, keepdims=True))
    alpha = jnp.exp2(m - m_new)
    p = jnp.exp2(s - m_new)
    l = alpha * l + jnp.sum(p, axis=-1, keepdims=True)
    acc = alpha * acc + pv(p.astype(BF16))
    return m_new, l, acc


def _flash_init(rows, dv):
    return (jnp.full((rows, 1), NEG_INF, F32), jnp.zeros((rows, 1), F32), jnp.zeros((rows, dv), F32))


def _mla_flash_kernel(q_ref, k_ref, v_ref, o_ref, *, t):
    qi = pl.program_id(2)
    heads = q_ref.shape[1]
    qs = [q_ref[0, hh] for hh in range(heads)]
    causal = lax.broadcasted_iota(jnp.int32, (t, t), 1) <= lax.broadcasted_iota(jnp.int32, (t, t), 0)

    def step(j, carry, mask):
        sl = pl.ds(pl.multiple_of(j * t, t), t)
        out = []
        for hh in range(heads):
            s = _dot_nt(qs[hh], k_ref[0, hh, sl, :])
            if mask is not None:
                s = jnp.where(mask, s, NEG_INF)
            out.append(_flash_update(*carry[hh], s, v_ref[0, hh, sl, :]))
        return tuple(out)

    init = tuple((jnp.full((t, 1), NEG_INF, F32), jnp.zeros((t, LANES), F32)) for _ in range(heads))
    carry = lax.fori_loop(0, qi, lambda j, c: step(j, c, None), init)
    carry = step(qi, carry, causal)
    o_ref[0] = jnp.concatenate([_flash_finish(acc, MLA_V) for _, acc in carry], axis=1).astype(o_ref.dtype)


def _mla_flash(q, k, v, t):
    nb, nh, ns, _ = q.shape
    hp = 4
    kern = functools.partial(_mla_flash_kernel, t=t)
    whole = pl.BlockSpec((1, hp, ns, LANES), lambda b, h, i: (b, h, 0, 0))
    return pl.pallas_call(
        kern, grid=(nb, nh // hp, ns // t),
        in_specs=[pl.BlockSpec((1, hp, t, LANES), lambda b, h, i: (b, h, i, 0)), whole, whole],
        out_specs=pl.BlockSpec((1, t, hp * MLA_V), lambda b, h, i: (b, i, h)),
        out_shape=jax.ShapeDtypeStruct((nb, ns, nh * MLA_V), BF16),
        compiler_params=_params(("arbitrary", "arbitrary", "arbitrary")), name="mla_flash",
    )(q, k, v)


def _masked_softmax(s, mask, axis):
    s = jnp.where(mask, s, NEG_INF)
    m = jnp.max(s, axis=axis, keepdims=True)
    e = jnp.where(mask, jnp.exp2(s - m), 0.0)
    return e / jnp.maximum(jnp.sum(e, axis=axis, keepdims=True), 1e-30)


def _softmax(s, axis):
    e = jnp.exp2(s - jnp.max(s, axis=axis, keepdims=True))
    return e / jnp.sum(e, axis=axis, keepdims=True)


def _top_blocks(imp, rounds):
    blk = lax.broadcasted_iota(jnp.int32, imp.shape, 0)
    sel = jnp.zeros(imp.shape, jnp.bool_)
    v = imp
    for _ in range(rounds):
        m = jnp.max(v, axis=0, keepdims=True)
        first = jnp.min(jnp.where(v == m, blk, imp.shape[0]), axis=0, keepdims=True)
        hit = blk == first
        sel = sel | hit
        v = jnp.where(hit, -jnp.inf, v)
    return sel


def _tile_heads(x, rep):
    return jnp.concatenate([x] * rep, axis=0)


def _nsa_prompt_kernel(q_ref, ka_ref, va_ref, win_ref, pool_ref, gate_ref, o_ref, *, tq, tk):
    qi = pl.program_id(1)
    nblk = pool_ref.shape[2]
    rep = NSA_REP
    rows = rep * tq
    start = qi * tq
    qpos4 = start + (lax.broadcasted_iota(jnp.int32, (rows, 1), 0) & (tq - 1))
    blk4 = lax.broadcasted_iota(jnp.int32, (rows, nblk), 1)
    blk_t = lax.broadcasted_iota(jnp.int32, (nblk, tq), 0)
    cur_t = (start + lax.broadcasted_iota(jnp.int32, (nblk, tq), 1)) // CMP_BLOCK
    blk_t4 = lax.broadcasted_iota(jnp.int32, (nblk, rows), 0)
    qpos_t4 = start + (lax.broadcasted_iota(jnp.int32, (nblk, rows), 1) & (tq - 1))
    gates = gate_ref[0]

    q_aug, o_cmp, o_win = [], [], []
    for g in range(NSA_GROUPS):
        q = q_ref[0, g * rep:(g + 1) * rep].reshape(rows, LANES)
        pool = pool_ref[0, g].astype(BF16)
        p_c = _masked_softmax(_dot_nt(q, pool), (blk4 + 1) * CMP_BLOCK - 1 <= qpos4, axis=1)
        o_cmp.append(_dot(p_c.astype(BF16), pool))
        p_t = _masked_softmax(_dot_nt(pool, q), (blk_t4 + 1) * CMP_BLOCK - 1 <= qpos_t4, axis=0)
        imp = p_t[:, 0:tq]
        for r in range(1, rep):
            imp = imp + p_t[:, r * tq:(r + 1) * tq]
        forced = (blk_t == 0) | (blk_t == cur_t) | (blk_t == cur_t - 1)
        imp = jnp.where(forced, jnp.inf, imp)
        imp = jnp.where(blk_t <= cur_t, imp, -jnp.inf)
        sel = _top_blocks(imp, SEL_TOPK) & (blk_t <= cur_t)
        bias = jnp.where(sel, 0.0, NEG_INF).T
        q_aug.append(jnp.concatenate([q, _tile_heads(bias, rep).astype(BF16)], axis=1))
        ntile = WINDOW // tq + 1
        scores, tiles = [], []
        for c in range(ntile):
            idx = qi - (ntile - 1) + c
            sl = pl.ds(pl.multiple_of(jnp.maximum(idx, 0) * tq, tq), tq)
            kvw = win_ref[0, g, sl, :]
            kp = idx * tq + lax.broadcasted_iota(jnp.int32, (1, tq), 1)
            ok = (kp <= qpos4) & (kp > qpos4 - WINDOW) & (kp >= 0)
            scores.append(jnp.where(ok, _dot_nt(q, kvw), NEG_INF))
            tiles.append(kvw)
        p_w = _softmax(jnp.concatenate(scores, axis=1), axis=1).astype(BF16)
        o_w = _dot(p_w[:, 0:tq], tiles[0])
        for c in range(1, ntile):
            o_w = o_w + _dot(p_w[:, c * tq:(c + 1) * tq], tiles[c])
        o_win.append(o_w)

    def sel_step(j, carry, causal):
        sl = pl.ds(pl.multiple_of(j * tk, tk), tk)
        out = []
        for g in range(NSA_GROUPS):
            s = _dot_nt(q_aug[g], ka_ref[0, g, sl, :])
            if causal:
                kpos = j * tk + lax.broadcasted_iota(jnp.int32, (1, tk), 1)
                s = jnp.where(kpos <= qpos4, s, NEG_INF)
            out.append(_flash_update(*carry[g], s, va_ref[0, g, sl, :]))
        return tuple(out)

    jd = start // tk
    init = tuple((jnp.full((rows, 1), NEG_INF, F32), jnp.zeros((rows, LANES), F32)) for _ in range(NSA_GROUPS))
    carry = lax.fori_loop(0, jd, lambda j, c: sel_step(j, c, False), init)
    carry = sel_step(jd, carry, True)

    outs = []
    for g in range(NSA_GROUPS):
        o_s = _flash_finish(carry[g][1], NSA_DIM)
        for r in range(rep):
            hh = g * rep + r
            rs = slice(r * tq, (r + 1) * tq)
            outs.append(gates[:, 3 * hh:3 * hh + 1] * o_cmp[g][rs, NSA_DIM:]
                        + gates[:, 3 * hh + 1:3 * hh + 2] * o_s[rs]
                        + gates[:, 3 * hh + 2:3 * hh + 3] * o_win[g][rs, NSA_DIM:])
    o_ref[0] = jnp.concatenate(outs, axis=1).astype(o_ref.dtype)


def _nsa_prompt(q, ka, va, win, pool, gates, tq, tk):
    nb, nh, ns, _ = q.shape
    nblk = ns // CMP_BLOCK
    kern = functools.partial(_nsa_prompt_kernel, tq=tq, tk=tk)
    whole = lambda a: pl.BlockSpec((1,) + a.shape[1:], lambda b, i: (b, 0, 0, 0))
    return pl.pallas_call(
        kern, grid=(nb, ns // tq),
        in_specs=[pl.BlockSpec((1, nh, tq, LANES), lambda b, i: (b, 0, i, 0)),
                  whole(ka), whole(va), whole(win),
                  pl.BlockSpec((1, NSA_GROUPS, nblk, LANES), lambda b, i: (b, 0, 0, 0)),
                  pl.BlockSpec((1, tq, LANES), lambda b, i: (b, i, 0))],
        out_specs=pl.BlockSpec((1, tq, nh * NSA_DIM), lambda b, i: (b, i, 0)),
        out_shape=jax.ShapeDtypeStruct((nb, ns, nh * NSA_DIM), BF16),
        compiler_params=_params(("arbitrary", "arbitrary")), name="nsa_prompt",
    )(q, ka, va, win, pool, gates)


def _head_matmul_kernel(a_ref, b_ref, o_ref):
    o_ref[0] = _dot(a_ref[0], b_ref[0]).astype(o_ref.dtype)


def _head_matmul(a, b, dtype):
    nh, m, k = a.shape
    n = b.shape[2]
    return pl.pallas_call(
        _head_matmul_kernel, grid=(nh,),
        in_specs=[pl.BlockSpec((1, m, k), lambda h: (h, 0, 0)), pl.BlockSpec((1, k, n), lambda h: (h, 0, 0))],
        out_specs=pl.BlockSpec((1, m, n), lambda h: (h, 0, 0)),
        out_shape=jax.ShapeDtypeStruct((nh, m, n), dtype),
        compiler_params=_params(("arbitrary",)), name="head_matmul",
    )(a, b)


def _new_token_mask(rows, t_new):
    tok = lax.broadcasted_iota(jnp.int32, (rows, LANES), 0) % t_new
    lane = lax.broadcasted_iota(jnp.int32, (rows, LANES), 1)
    return (lane < t_new) & (lane <= tok)


def _page_specs(pages, feat):
    return [pl.BlockSpec((1, feat, PAGE_SIZE), lambda b, j, pt, p=p: (pt[b, j * pages + p], 0, 0))
            for p in range(pages)]


def _mla_sample_kernel(pt_ref, q_ref, new_ref, *rest, pages, t_new):
    page_refs, (o_ref, kv_scr, new_scr, m_scr, l_scr, acc_scr) = rest[:pages], rest[pages:]
    j = pl.program_id(1)
    width = kv_scr.shape[0]

    @pl.when((pl.program_id(0) == 0) & (j == 0))
    def _():
        kv_scr[...] = jnp.zeros(kv_scr.shape, BF16)

    @pl.when(j == 0)
    def _():
        m0, l0, a0 = _flash_init(q_ref.shape[1], width)
        m_scr[...], l_scr[...], acc_scr[...] = m0, l0, a0

    for p in range(pages):
        kv_scr[0:MLA_ROW, p * PAGE_SIZE:(p + 1) * PAGE_SIZE] = page_refs[p][0].astype(BF16)
    q = q_ref[0]
    kv = kv_scr[...]
    carry = _online_update((m_scr[...], l_scr[...], acc_scr[...]), _dot(q, kv), lambda p: _dot_nt(p, kv))
    m_scr[...], l_scr[...], acc_scr[...] = carry

    @pl.when(j == pl.num_programs(1) - 1)
    def _():
        new_scr[...] = jnp.zeros(new_scr.shape, BF16)
        new_scr[0:t_new, 0:MLA_ROW] = new_ref[0].astype(BF16)
        kvn = new_scr[...]
        s = jnp.where(_new_token_mask(q.shape[0], t_new), _dot_nt(q, kvn), NEG_INF)
        _, l, acc = _online_update(carry, s, lambda p: _dot(p, kvn))
        o_ref[0] = acc / l


def _mla_sample(page_table, q, new_rows, cache_t, pages):
    nb, rows, width = q.shape
    t_new = new_rows.shape[1]
    n_pages = page_table.shape[1]
    kern = functools.partial(_mla_sample_kernel, pages=pages, t_new=t_new)
    grid_spec = pltpu.PrefetchScalarGridSpec(
        num_scalar_prefetch=1, grid=(nb, n_pages // pages),
        in_specs=[pl.BlockSpec((1, rows, width), lambda b, j, pt: (b, 0, 0)),
                  pl.BlockSpec((1, t_new, MLA_ROW), lambda b, j, pt: (b, 0, 0))] + _page_specs(pages, MLA_ROW),
        out_specs=pl.BlockSpec((1, rows, width), lambda b, j, pt: (b, 0, 0)),
        scratch_shapes=[pltpu.VMEM((width, pages * PAGE_SIZE), BF16), pltpu.VMEM((LANES, width), BF16),
                        pltpu.VMEM((rows, 1), F32), pltpu.VMEM((rows, 1), F32), pltpu.VMEM((rows, width), F32)])
    return pl.pallas_call(
        kern, grid_spec=grid_spec, out_shape=jax.ShapeDtypeStruct((nb, rows, width), F32),
        compiler_params=_params(("arbitrary", "arbitrary")), name="mla_sample",
    )(page_table, q, new_rows, *([cache_t] * pages))


def _cmp_pool_kernel(pt_ref, w_ref, e_ref, *rest, pages):
    page_refs, (o_ref, hi_scr, lo_scr) = rest[:pages], rest[pages:]
    w = w_ref[...]
    for p in range(pages):
        x = page_refs[p][0] * w
        hi = x.astype(BF16)
        sl = slice(p * PAGE_SIZE, (p + 1) * PAGE_SIZE)
        hi_scr[:, sl] = hi
        lo_scr[:, sl] = (x - hi.astype(F32)).astype(BF16)
    e = e_ref[...]
    o_ref[0] = _dot_nt(e, hi_scr[...]) + _dot_nt(e, lo_scr[...])


def _cmp_pool(page_table, w_pool_t, cache_t, pages):
    nb, n_pages = page_table.shape
    feat = cache_t.shape[1]
    per_step = pages * PAGE_SIZE // CMP_BLOCK
    expand = _block_expand(per_step, pages * PAGE_SIZE)
    kern = functools.partial(_cmp_pool_kernel, pages=pages)
    grid_spec = pltpu.PrefetchScalarGridSpec(
        num_scalar_prefetch=1, grid=(nb, n_pages // pages),
        in_specs=[pl.BlockSpec(w_pool_t.shape, lambda b, j, pt: (0, 0)),
                  pl.BlockSpec(expand.shape, lambda b, j, pt: (0, 0))] + _page_specs(pages, feat),
        out_specs=pl.BlockSpec((1, per_step, feat), lambda b, j, pt: (b, j, 0)),
        scratch_shapes=[pltpu.VMEM((feat, pages * PAGE_SIZE), BF16), pltpu.VMEM((feat, pages * PAGE_SIZE), BF16)])
    return pl.pallas_call(
        kern, grid_spec=grid_spec,
        out_shape=jax.ShapeDtypeStruct((nb, n_pages * PAGE_SIZE // CMP_BLOCK, feat), F32),
        compiler_params=_params(("arbitrary", "arbitrary")), name="cmp_pool",
    )(page_table, w_pool_t, expand, *([cache_t] * pages))


def _pack_group(rows, g):
    ko, vo = g * NSA_DIM, NSA_GROUPS * NSA_DIM + g * NSA_DIM
    return jnp.concatenate([rows[:, ko:ko + NSA_DIM], rows[:, vo:vo + NSA_DIM]], axis=1).astype(BF16)


def _pack_group_t(feats, g):
    ko, vo = g * NSA_DIM, NSA_GROUPS * NSA_DIM + g * NSA_DIM
    return jnp.concatenate([feats[ko:ko + NSA_DIM], feats[vo:vo + NSA_DIM]], axis=0).astype(BF16)


def _nsa_sample_kernel(pt_ref, q_ref, pool_ref, gate_ref, win_ref, wnew_ref, snew_ref, expand_ref, *rest,
                       pages, t_new):
    page_refs = rest[:pages]
    o_ref, kv_scr, new_scr, q_scr, sel_scr, oc_scr, ow_scr, m_scr, l_scr, acc_scr = rest[pages:]
    j = pl.program_id(1)
    rep = NSA_REP
    rows = rep * t_new
    nblk = pool_ref.shape[1]
    tk = pages * PAGE_SIZE
    new_ok = _new_token_mask(rows, t_new)

    def new_tile(ref, g):
        new_scr[...] = jnp.zeros(new_scr.shape, BF16)
        new_scr[0:t_new, :] = _pack_group(ref[0], g)
        return new_scr[...]

    @pl.when(j == 0)
    def _():
        pool = pool_ref[0]
        win_t = win_ref[0]
        wb = win_t.shape[1]
        tok = lax.broadcasted_iota(jnp.int32, (rows, 1), 0) % t_new
        lane = lax.broadcasted_iota(jnp.int32, (nblk, LANES), 1)
        blk = lax.broadcasted_iota(jnp.int32, (nblk, LANES), 0)
        for g in range(NSA_GROUPS):
            q = q_ref[0, g * rows:(g + 1) * rows]
            pg = _pack_group(pool, g)
            oc_scr[g] = _dot(_softmax(_dot_nt(q, pg), axis=1).astype(BF16), pg)
            q_scr[...] = jnp.zeros(q_scr.shape, BF16)
            q_scr[0:rows, :] = q
            p_t = _softmax(_dot_nt(pg, q_scr[...]), axis=0)
            tot = p_t
            for r in range(1, rep):
                tot = tot + pltpu.roll(p_t, LANES - r * t_new, 1)
            tot = jnp.where(lane < t_new, tot, 0.0)
            imp = tot
            for r in range(1, rep):
                imp = imp + pltpu.roll(tot, r * t_new, 1)
            imp = jnp.where((blk == 0) | (blk == nblk - 1), jnp.inf, imp)
            sel_scr[g] = jnp.where(_top_blocks(imp, SEL_TOPK - 1), 1.0, 0.0).T[0:rows]
            wg = _pack_group_t(win_t, g)
            i = lax.broadcasted_iota(jnp.int32, (1, wb), 1)
            ng = new_tile(wnew_ref, g)
            s_w = jnp.concatenate([jnp.where(i > tok + (wb - WINDOW), _dot(q, wg), NEG_INF),
                                   jnp.where(new_ok, _dot_nt(q, ng), NEG_INF)], axis=1)
            p_w = _softmax(s_w, axis=1).astype(BF16)
            ow_scr[g] = _dot_nt(p_w[:, :wb], wg) + _dot(p_w[:, wb:], ng)
            m0, l0, a0 = _flash_init(rows, LANES)
            m_scr[g], l_scr[g], acc_scr[g] = m0, l0, a0

    for p in range(pages):
        page = page_refs[p][0]
        for g in range(NSA_GROUPS):
            kv_scr[g, :, p * PAGE_SIZE:(p + 1) * PAGE_SIZE] = _pack_group_t(page, g)
    sl = pl.ds(pl.multiple_of(j * tk, tk), tk)
    for g in range(NSA_GROUPS):
        q = q_ref[0, g * rows:(g + 1) * rows]
        kv = kv_scr[g]
        hit = _dot(sel_scr[g].astype(BF16), expand_ref[:, sl])
        s = _dot(q, kv) + (hit - 1.0) * (-NEG_INF)
        carry = _online_update((m_scr[g], l_scr[g], acc_scr[g]), s, lambda p, kv=kv: _dot_nt(p, kv))
        m_scr[g], l_scr[g], acc_scr[g] = carry

    @pl.when(j == pl.num_programs(1) - 1)
    def _():
        gates = gate_ref[0]
        for g in range(NSA_GROUPS):
            q = q_ref[0, g * rows:(g + 1) * rows]
            ng = new_tile(snew_ref, g)
            s = jnp.where(new_ok, _dot_nt(q, ng), NEG_INF)
            _, l, acc = _online_update((m_scr[g], l_scr[g], acc_scr[g]), s, lambda p, ng=ng: _dot(p, ng))
            gg = gates[g * rows:(g + 1) * rows]
            o_ref[0, g * rows:(g + 1) * rows, :] = (gg[:, 0:1] * oc_scr[g] + gg[:, 1:2] * (acc / l)
                                                    + gg[:, 2:3] * ow_scr[g])


def _nsa_sample(page_table, q, pool, gates, win_t, win_new, slc_new, expand, cache_t, pages):
    nb, rows2, _ = q.shape
    t_new = slc_new.shape[1]
    n_pages = page_table.shape[1]
    feat = cache_t.shape[1]
    rows = rows2 // NSA_GROUPS
    per_b = lambda shape: pl.BlockSpec((1,) + shape, lambda b, j, pt: (b, 0, 0))
    kern = functools.partial(_nsa_sample_kernel, pages=pages, t_new=t_new)
    grp = lambda *shape: pltpu.VMEM((NSA_GROUPS,) + shape, F32)
    grid_spec = pltpu.PrefetchScalarGridSpec(
        num_scalar_prefetch=1, grid=(nb, n_pages // pages),
        in_specs=[per_b((rows2, LANES)), per_b(pool.shape[1:]), per_b((rows2, LANES)), per_b(win_t.shape[1:]),
                  per_b((t_new, feat)), per_b((t_new, feat)),
                  pl.BlockSpec(expand.shape, lambda b, j, pt: (0, 0))] + _page_specs(pages, feat),
        out_specs=per_b((rows2, LANES)),
        scratch_shapes=[pltpu.VMEM((NSA_GROUPS, LANES, pages * PAGE_SIZE), BF16), pltpu.VMEM((LANES, LANES), BF16),
                        pltpu.VMEM((LANES, LANES), BF16), grp(rows, expand.shape[0]),
                        grp(rows, LANES), grp(rows, LANES), grp(rows, 1), grp(rows, 1), grp(rows, LANES)])
    return pl.pallas_call(
        kern, grid_spec=grid_spec, out_shape=jax.ShapeDtypeStruct((nb, rows2, LANES), F32),
        compiler_params=_params(("arbitrary", "arbitrary")), name="nsa_sample",
    )(page_table, q, pool, gates, win_t, win_new, slc_new, expand, *([cache_t] * pages))


def _tail_kernel(x_ref, ya_ref, yb_ref, ga_ref, gb_ref, p_ref, wba_ref, wbn_ref, wo_ref, lnf_ref,
                 wg_ref, wu_ref, wd_ref, lnp_ref, wpg_ref, wpp_ref, lnl_ref, o_ref, *, chunk):
    merged = ga_ref[...] * _dot(ya_ref[...], wba_ref[...]) + gb_ref[...] * _dot(yb_ref[...], wbn_ref[...])
    x = x_ref[...] + _dot(merged.astype(BF16), wo_ref[...])
    h = _rms(x, lnf_ref[...]).astype(BF16)

    def ffn(c, acc):
        sl = pl.ds(pl.multiple_of(c * chunk, chunk), chunk)
        hidden = jax.nn.silu(_dot(h, wg_ref[:, sl])) * _dot(h, wu_ref[:, sl])
        return acc + _dot(hidden.astype(BF16), wd_ref[sl, :])

    x = x + lax.fori_loop(0, wg_ref.shape[1] // chunk, ffn, jnp.zeros(x.shape, F32))
    gate = jax.nn.sigmoid(_dot(_rms(x, lnp_ref[...]).astype(BF16), wpg_ref[...]))
    x = x + gate * _dot(p_ref[...].astype(BF16), wpp_ref[...])
    o_ref[...] = _rms(x, lnl_ref[...])


def _tail(x, ya, yb, ga, gb, p, weights, tm):
    n, d_model = x.shape
    row = lambda a: pl.BlockSpec((tm, a.shape[1]), lambda i: (i, 0))
    acts = (x, ya, yb, ga, gb, p)
    kern = functools.partial(_tail_kernel, chunk=2 * LANES)
    return pl.pallas_call(
        kern, grid=(n // tm,),
        in_specs=[row(a) for a in acts] + [_resident(w.shape) for w in weights],
        out_specs=pl.BlockSpec((tm, d_model), lambda i: (i, 0)),
        out_shape=jax.ShapeDtypeStruct((n, d_model), F32),
        compiler_params=_params(("arbitrary",)), name="tail",
    )(*acts, *weights)


def _rope_tables(pos):
    pos = pos.astype(F32)[:, None]
    n = pos.shape[0]

    def cs(rot):
        half = rot // 2
        inv = jnp.float32(ROPE_THETA) ** (-jnp.arange(half, dtype=F32) * 2.0 / rot)
        ang = pos * inv[None, :]
        return jnp.cos(ang), jnp.sin(ang)

    one = lambda w: jnp.ones((n, w), F32)
    zero = lambda w: jnp.zeros((n, w), F32)
    c, s = cs(MLA_ROPE)
    pad = LANES - MLA_NOPE - MLA_ROPE
    cm = jnp.concatenate([one(MLA_NOPE), c, c, one(pad)], axis=1)
    sm = jnp.concatenate([zero(MLA_NOPE), -s, s, zero(pad)], axis=1)
    c, s = cs(NSA_ROT)
    rest = NSA_DIM - NSA_ROT
    cn = jnp.concatenate([c, c, one(rest)] * (LANES // NSA_DIM), axis=1)
    sn = jnp.concatenate([-s, s, zero(rest)] * (LANES // NSA_DIM), axis=1)
    return cm, sm, cn, sn


def _layout_w_in(w_in):
    d_model = w_in.shape[0]
    sizes = (MLA_HEADS * (MLA_NOPE + MLA_ROPE), MLA_KV_LORA, MLA_ROPE, NSA_HEADS * NSA_DIM,
             SEG_KV // 3, SEG_KV // 3, SEG_KV // 3, 3 * NSA_HEADS, 2 * d_model)
    offs = np.concatenate([[0], np.cumsum(sizes)])
    qa, ca, ra, qb, zc, zs, zw, gn, gm = [w_in[:, int(offs[i]):int(offs[i + 1])] for i in range(len(sizes))]
    qa = qa.reshape(d_model, MLA_HEADS, MLA_NOPE + MLA_ROPE)
    qa = jnp.pad(qa, ((0, 0), (0, 0), (0, LANES - MLA_NOPE - MLA_ROPE))).reshape(d_model, SEG_QA)
    ra = jnp.pad(ra, ((0, 0), (MLA_NOPE, LANES - MLA_NOPE - MLA_ROPE)))
    gn = jnp.pad(gn, ((0, 0), (0, SEG_GN - gn.shape[1])))
    return jnp.concatenate([qa, ca, ra, qb, zc, zs, zw, gn, gm], axis=1).astype(BF16)


def _layout_w_up(w_uk, w_uv):
    wk = jnp.pad(w_uk, ((0, 0), (0, 0), (0, LANES - MLA_NOPE))).reshape(MLA_KV_LORA, SEG_QA)
    return jnp.concatenate([wk, w_uv.reshape(MLA_KV_LORA, MLA_HEADS * MLA_V)], axis=1).astype(BF16)


def _layout_absorb(w_uk, w_uv, width):
    to_lat = jnp.zeros((MLA_HEADS, LANES, width), F32)
    to_lat = to_lat.at[:, 0:MLA_NOPE, 0:MLA_KV_LORA].set(w_uk.transpose(1, 2, 0))
    to_lat = to_lat.at[:, MLA_NOPE:MLA_NOPE + MLA_ROPE, MLA_KV_LORA:MLA_ROW].set(jnp.eye(MLA_ROPE, dtype=F32))
    to_val = jnp.zeros((MLA_HEADS, width, MLA_V), F32).at[:, 0:MLA_KV_LORA, :].set(w_uv.transpose(1, 0, 2))
    return to_lat.astype(BF16), to_val.astype(BF16)


def _block_expand(nblk, nkeys):
    return (jnp.arange(nkeys)[None, :] // CMP_BLOCK == jnp.arange(nblk)[:, None]).astype(BF16)


def kernel(x_prompt, x_sample, cache_mla, cache_nsa_cmp, cache_nsa_slc, state_nsa_win, page_table, p_prompt, p_sample, ln_attn, w_in, mla_kv_norm, mla_w_uk, mla_w_uv, nsa_w_cmp_k, nsa_w_cmp_v, w_branch_mla, w_branch_nsa, w_out, ln_ffn, w_ffn_gate, w_ffn_up, w_ffn_down, ln_ple, w_ple_gate, w_ple_proj, ln_final):
    nb, ns, d_model = x_prompt.shape
    db, t_new, _ = x_sample.shape
    depth = w_in.shape[0]
    n_pages = page_table.shape[1]
    past = n_pages * PAGE_SIZE
    wb = state_nsa_win.shape[2]
    width = 2 * NSA_GROUPS * NSA_DIM
    g_, d_ = NSA_GROUPS, NSA_DIM
    assert depth == 1 and t_new <= CMP_BLOCK and past % CMP_BLOCK == 0 and wb == WINDOW and past >= wb
    assert ns % 512 == 0 and (db * t_new) % 512 == 0 and NSA_REP * t_new <= LANES
    tm = 512
    pages = min(64, n_pages)
    lat_w = 3 * LANES
    i = 0

    w_all = _layout_w_in(w_in[i])
    w_up = _layout_w_up(mla_w_uk[i], mla_w_uv[i])
    to_lat, to_val = _layout_absorb(mla_w_uk[i], mla_w_uv[i], lat_w)
    w_pool = jnp.concatenate([nsa_w_cmp_k[i]] * g_ + [nsa_w_cmp_v[i]] * g_, axis=1)
    w_pool_t = jnp.tile(w_pool.T, (1, PAGE_SIZE // CMP_BLOCK))
    ln_a = ln_attn[i][None, :]
    kvn = mla_kv_norm[i][None, :]
    tail_w = (w_branch_mla[i].astype(BF16), w_branch_nsa[i].astype(BF16), w_out[i].astype(BF16),
              ln_ffn[i][None, :], w_ffn_gate[i].astype(BF16), w_ffn_up[i].astype(BF16),
              w_ffn_down[i].astype(BF16), ln_ple[i][None, :], w_ple_gate[i].astype(BF16),
              w_ple_proj[i].astype(BF16), ln_final[None, :])

    xp = x_prompt.reshape(nb * ns, d_model)
    nblk = ns // CMP_BLOCK
    (qm, km, vm, mla_p, qn, kvc_p, kvs_p, kvw_p, ka, va, win, pool, gn, ga, gb) = _proj(
        xp, _rope_tables(jnp.arange(ns)), ln_a, w_all, w_up, kvn, w_pool, nb, ns, tm, nblk)
    ya = _mla_flash(qm, km, vm, 512)
    pool5 = pool.reshape(nb, nblk, 2, g_, d_)
    pool_g = jnp.concatenate([pool5[:, :, 0].transpose(0, 2, 1, 3), pool5[:, :, 1].transpose(0, 2, 1, 3)], axis=-1)
    yb = _nsa_prompt(qn, ka, va, win, pool_g, gn.reshape(nb, ns, LANES), 128, 512)
    y_prompt = _tail(xp, ya.reshape(nb * ns, -1), yb.reshape(nb * ns, -1), ga, gb,
                     p_prompt[i].reshape(nb * ns, -1), tail_w, tm).reshape(nb, ns, d_model)

    n_s = db * t_new
    xs = x_sample.reshape(n_s, d_model)
    pos_s = jnp.tile(past + jnp.arange(t_new), db)
    (qm, _, _, mla_s, qn, kvc_s, kvs_s, kvw_s, _, _, _, _, gn, ga, gb) = _proj(
        xs, _rope_tables(pos_s), ln_a, w_all, w_up, kvn, w_pool, 1, n_s, tm, LANES)
    by_batch = lambda a: a.reshape(a.shape[0], db, t_new, a.shape[-1]).transpose(1, 0, 2, 3).reshape(
        db, a.shape[0] * t_new, a.shape[-1])
    q_lat = by_batch(_head_matmul(qm[0], to_lat, BF16))
    mla_t = cache_mla[i].transpose(0, 2, 1)
    o_lat = _mla_sample(page_table, q_lat, mla_s.reshape(db, t_new, MLA_ROW), mla_t, pages)
    o_lat = o_lat.reshape(db, MLA_HEADS, t_new, lat_w).transpose(1, 0, 2, 3).reshape(MLA_HEADS, n_s, lat_w)
    ya = _head_matmul(o_lat.astype(BF16), to_val, BF16).transpose(1, 0, 2).reshape(n_s, MLA_HEADS * MLA_V)

    cmp_t = cache_nsa_cmp[i].reshape(-1, PAGE_SIZE, width).transpose(0, 2, 1)
    slc_t = cache_nsa_slc[i].reshape(-1, PAGE_SIZE, width).transpose(0, 2, 1)
    pool_s = _cmp_pool(page_table, w_pool_t, cmp_t, pages)
    gate_s = gn[:, :3 * NSA_HEADS].reshape(db, t_new, NSA_HEADS, 3).transpose(0, 2, 1, 3).reshape(
        db, NSA_HEADS * t_new, 3)
    gate_s = jnp.pad(gate_s, ((0, 0), (0, 0), (0, LANES - 3)))
    win_state = state_nsa_win[i].reshape(db, wb, width)
    o_nsa = _nsa_sample(page_table, by_batch(qn[0]), pool_s, gate_s, win_state.transpose(0, 2, 1),
                        kvw_s.reshape(db, t_new, width), kvs_s.reshape(db, t_new, width),
                        _block_expand(past // CMP_BLOCK, past), slc_t, pages)
    yb = o_nsa[:, :, NSA_DIM:].reshape(db, NSA_HEADS, t_new, d_).transpose(0, 2, 1, 3).reshape(n_s, -1)
    y_sample = _tail(xs, ya, yb.astype(BF16), ga, gb, p_sample[i].reshape(n_s, -1), tail_w, tm).reshape(
        db, t_new, d_model)

    kv5 = lambda a, b, s: a.reshape(1, b, s, 2, g_, d_)
    win_all = jnp.concatenate([win_state, kvw_s.reshape(db, t_new, width)], axis=1)
    wkeep = min(WINDOW, ns)
    return (y_prompt, y_sample,
            mla_p.reshape(1, nb, ns, MLA_ROW), mla_s.reshape(1, db, t_new, MLA_ROW),
            kv5(kvc_p, nb, ns), kv5(kvc_s, db, t_new), kv5(kvs_p, nb, ns), kv5(kvs_s, db, t_new),
            kv5(kvw_p, nb, ns)[:, :, ns - wkeep:],
            kv5(win_all[:, -min(WINDOW, past + t_new):], db, min(WINDOW, past + t_new)))
```

```python
import functools
import math

import jax
import jax.numpy as jnp
import numpy as np
from jax import lax
from jax.experimental import pallas as pl
from jax.experimental.pallas import tpu as pltpu

F32 = jnp.float32
BF16 = jnp.bfloat16

NORM_EPS = 1e-6
ROPE_THETA = 500000.0
NEG_INF = -1e30
LOG2E = math.log2(math.e)
PAGE_SIZE = 128
MLA_HEADS = 8
MLA_NOPE = 64
MLA_ROPE = 32
MLA_V = 64
MLA_KV_LORA = 256
MLA_ROW = MLA_KV_LORA + MLA_ROPE
NSA_HEADS = 8
NSA_GROUPS = 2
NSA_REP = NSA_HEADS // NSA_GROUPS
NSA_DIM = 64
NSA_ROT = NSA_DIM // 4
CMP_BLOCK = 64
SEL_TOPK = 16
WINDOW = 512
LANES = 128
VMEM_LIMIT = 60 * 1024 * 1024

SEG_QA = MLA_HEADS * LANES
SEG_C = MLA_KV_LORA
SEG_R = LANES
SEG_QB = NSA_HEADS * NSA_DIM
SEG_KV = 3 * 2 * NSA_GROUPS * NSA_DIM
SEG_GN = LANES
OFF_C = SEG_QA
OFF_R = OFF_C + SEG_C
OFF_QB = OFF_R + SEG_R
OFF_KV = OFF_QB + SEG_QB
OFF_GN = OFF_KV + SEG_KV
OFF_GM = OFF_GN + SEG_GN


def _params(sem):
    return pltpu.CompilerParams(dimension_semantics=sem, vmem_limit_bytes=VMEM_LIMIT)


def _resident(shape):
    nd = len(shape)
    return pl.BlockSpec(shape, lambda *_: (0,) * nd, pipeline_mode=pl.Buffered(1))


def _dot(a, b):
    return jnp.dot(a, b, preferred_element_type=F32)


def _dot_nt(a, b):
    return lax.dot_general(a, b, (((1,), (1,)), ((), ())), preferred_element_type=F32)


def _rms(x, g):
    return x * lax.rsqrt(jnp.mean(x * x, axis=-1, keepdims=True) + NORM_EPS) * g


def _value_slot(v, rows):
    lane = lax.broadcasted_iota(jnp.int32, (rows, LANES - v.shape[1]), 1)
    return jnp.concatenate([v, jnp.where(lane == 0, 1.0, 0.0)], axis=1).astype(BF16)


def _proj_kernel(x_ref, ln_ref, w_ref, wup_ref, kvn_ref, cm_ref, sm_ref, cn_ref, sn_ref, wpool_ref,
                 qm_ref, km_ref, vm_ref, mla_ref, qn_ref, kvc_ref, kvs_ref, kvw_ref,
                 ka_ref, va_ref, win_ref, pool_ref, gn_ref, ga_ref, gb_ref, *, nt):
    tm = x_ref.shape[0]
    d_model = x_ref.shape[1]
    nblk = ka_ref.shape[3] - LANES
    h = _rms(x_ref[...], ln_ref[...]).astype(BF16)
    lane = lax.broadcasted_iota(jnp.int32, (tm, LANES), 1)
    cm, sm, cn, sn = cm_ref[...], sm_ref[...], cn_ref[...], sn_ref[...]

    def rope_m(z):
        half = MLA_ROPE // 2
        sw = jnp.where(lane < MLA_NOPE + half, pltpu.roll(z, LANES - half, 1), pltpu.roll(z, half, 1))
        return z * cm + sw * sm

    def rope_n(z):
        half = NSA_ROT // 2
        sw = jnp.where((lane & (NSA_DIM - 1)) < half, pltpu.roll(z, LANES - half, 1), pltpu.roll(z, half, 1))
        return z * cn + sw * sn

    zq = _dot(h, w_ref[:, 0:SEG_QA])
    scale_m = (MLA_NOPE + MLA_ROPE) ** -0.5 * LOG2E
    for hh in range(MLA_HEADS):
        qm_ref[0, hh] = (rope_m(zq[:, hh * LANES:(hh + 1) * LANES]) * scale_m).astype(BF16)

    zc = _dot(h, w_ref[:, OFF_C:OFF_R])
    c = _rms(zc, kvn_ref[...])
    kpe = rope_m(_dot(h, w_ref[:, OFF_R:OFF_QB]))
    mla_ref[0, 0:MLA_KV_LORA, :] = c.T
    mla_ref[0, MLA_KV_LORA:MLA_ROW, :] = kpe.T[MLA_NOPE:MLA_NOPE + MLA_ROPE]
    up = _dot(c.astype(BF16), wup_ref[...])
    for hh in range(MLA_HEADS):
        km_ref[0, hh] = (up[:, hh * LANES:(hh + 1) * LANES] + kpe).astype(BF16)
        vo = SEG_QA + hh * MLA_V
        vm_ref[0, hh] = _value_slot(up[:, vo:vo + MLA_V], tm)

    zqb = _dot(h, w_ref[:, OFF_QB:OFF_KV])
    zero_half = jnp.zeros((tm, NSA_DIM), F32)
    scale_n = NSA_DIM ** -0.5 * LOG2E
    for pair in range(NSA_HEADS // 2):
        zz = rope_n(zqb[:, pair * LANES:(pair + 1) * LANES]) * scale_n
        qn_ref[0, 2 * pair] = jnp.concatenate([zz[:, :NSA_DIM], zero_half], axis=1).astype(BF16)
        qn_ref[0, 2 * pair + 1] = jnp.concatenate([zz[:, NSA_DIM:], zero_half], axis=1).astype(BF16)

    zkv = _dot(h, w_ref[:, OFF_KV:OFF_GN])
    width = 2 * NSA_GROUPS * NSA_DIM

    def kv_rows(i):
        z = zkv[:, i * width:(i + 1) * width]
        return jnp.concatenate([rope_n(z[:, :LANES]), z[:, LANES:]], axis=1)

    rows_c, rows_s, rows_w = kv_rows(0), kv_rows(1), kv_rows(2)
    kvc_ref[0] = rows_c.T
    kvs_ref[0] = rows_s.T
    kvw_ref[0] = rows_w.T
    pos = (pl.program_id(0) % nt) * tm + lax.broadcasted_iota(jnp.int32, (tm, nblk), 0)
    onehot = jnp.where(lax.broadcasted_iota(jnp.int32, (tm, nblk), 1) == pos // CMP_BLOCK, 1.0, 0.0)
    for g in range(NSA_GROUPS):
        ko, vo = g * NSA_DIM, LANES + g * NSA_DIM
        ka_ref[0, g] = jnp.concatenate([rows_s[:, ko:ko + NSA_DIM], zero_half, onehot], axis=1).astype(BF16)
        va_ref[0, g] = _value_slot(rows_s[:, vo:vo + NSA_DIM], tm)
        win_ref[0, g] = jnp.concatenate([rows_w[:, ko:ko + NSA_DIM], rows_w[:, vo:vo + NSA_DIM]],
                                        axis=1).astype(BF16)
    pool_ref[...] = jnp.sum(rows_c.reshape(tm // CMP_BLOCK, CMP_BLOCK, width) * wpool_ref[...][None], axis=1)

    gn_ref[...] = jax.nn.sigmoid(_dot(h, w_ref[:, OFF_GN:OFF_GM]))
    zm = _dot(h, w_ref[:, OFF_GM:OFF_GM + 2 * d_model])
    ga_ref[...] = jax.nn.sigmoid(zm[:, :d_model])
    gb_ref[...] = jax.nn.sigmoid(zm[:, d_model:])


def _proj(x, tabs, ln, w_all, w_up, kv_norm, w_pool, nb, ns, tm, nblk):
    n, d_model = x.shape
    nt = ns // tm
    width = 2 * NSA_GROUPS * NSA_DIM
    row = lambda w: pl.BlockSpec((tm, w), lambda i: (i, 0))
    head = lambda hn, w: pl.BlockSpec((1, hn, tm, w), lambda i: (i // nt, 0, i % nt, 0))
    tab = pl.BlockSpec((tm, LANES), lambda i: (i % nt, 0))
    feat = lambda f: pl.BlockSpec((1, f, tm), lambda i: (i // nt, 0, i % nt))
    slots = (
        (head(MLA_HEADS, LANES), (nb, MLA_HEADS, ns, LANES), BF16),
        (head(MLA_HEADS, LANES), (nb, MLA_HEADS, ns, LANES), BF16),
        (head(MLA_HEADS, LANES), (nb, MLA_HEADS, ns, LANES), BF16),
        (feat(MLA_ROW), (nb, MLA_ROW, ns), F32),
        (head(NSA_HEADS, LANES), (nb, NSA_HEADS, ns, LANES), BF16),
        (feat(width), (nb, width, ns), F32),
        (feat(width), (nb, width, ns), F32),
        (feat(width), (nb, width, ns), F32),
        (head(NSA_GROUPS, LANES + nblk), (nb, NSA_GROUPS, ns, LANES + nblk), BF16),
        (head(NSA_GROUPS, LANES), (nb, NSA_GROUPS, ns, LANES), BF16),
        (head(NSA_GROUPS, LANES), (nb, NSA_GROUPS, ns, LANES), BF16),
        (pl.BlockSpec((tm // CMP_BLOCK, width), lambda i: (i, 0)), (n // CMP_BLOCK, width), F32),
        (row(LANES), (n, LANES), F32),
        (row(d_model), (n, d_model), F32),
        (row(d_model), (n, d_model), F32),
    )
    in_specs = [row(d_model), _resident(ln.shape), _resident(w_all.shape), _resident(w_up.shape),
                _resident(kv_norm.shape), tab, tab, tab, tab, _resident(w_pool.shape)]
    return pl.pallas_call(
        functools.partial(_proj_kernel, nt=nt), grid=(n // tm,), in_specs=in_specs,
        out_specs=tuple(s[0] for s in slots),
        out_shape=tuple(jax.ShapeDtypeStruct(s[1], s[2]) for s in slots),
        compiler_params=_params(("arbitrary",)), name="proj",
    )(x, ln, w_all, w_up, kv_norm, *tabs, w_pool)


def _flash_update(m, acc, s, v):
    m_new = jnp.maximum(m, jnp.max(s, axis=-1, keepdims=True))
    alpha = jnp.exp2(m - m_new)
    p = jnp.exp2(s - m_new).astype(BF16)
    return m_new, alpha * acc + _dot(p, v)


def _flash_finish(acc, dv):
    return acc[:, :dv] / acc[:, dv:dv + 1]


def _online_update(carry, s, pv):
    m, l, acc = carry
    m_new = jnp.maximum(m, jnp.max(s, axis=-1, keepdims=True))
    alpha = jnp.exp2(m - m_new)
    p = jnp.exp2(s - m_new)
    l = alpha * l + jnp.sum(p, axis=-1, keepdims=True)
    acc = alpha * acc + pv(p.astype(BF16))
    return m_new, l, acc


def _flash_init(rows, dv):
    return (jnp.full((rows, 1), NEG_INF, F32), jnp.zeros((rows, 1), F32), jnp.zeros((rows, dv), F32))


def _mla_flash_kernel(q_ref, k_ref, v_ref, o_ref, *, t):
    qi = pl.program_id(2)
    heads = q_ref.shape[1]
    qs = [q_ref[0, hh] for hh in range(heads)]
    causal = lax.broadcasted_iota(jnp.int32, (t, t), 1) <= lax.broadcasted_iota(jnp.int32, (t, t), 0)

    def step(j, carry, mask):
        sl = pl.ds(pl.multiple_of(j * t, t), t)
        out = []
        for hh in range(heads):
            s = _dot_nt(qs[hh], k_ref[0, hh, sl, :])
            if mask is not None:
                s = jnp.where(mask, s, NEG_INF)
            out.append(_flash_update(*carry[hh], s, v_ref[0, hh, sl, :]))
        return tuple(out)

    init = tuple((jnp.full((t, 1), NEG_INF, F32), jnp.zeros((t, LANES), F32)) for _ in range(heads))
    carry = lax.fori_loop(0, qi, lambda j, c: step(j, c, None), init)
    carry = step(qi, carry, causal)
    o_ref[0] = jnp.concatenate([_flash_finish(acc, MLA_V) for _, acc in carry], axis=1).astype(o_ref.dtype)


def _mla_flash(q, k, v, t):
    nb, nh, ns, _ = q.shape
    hp = 4
    kern = functools.partial(_mla_flash_kernel, t=t)
    whole = pl.BlockSpec((1, hp, ns, LANES), lambda b, h, i: (b, h, 0, 0))
    return pl.pallas_call(
        kern, grid=(nb, nh // hp, ns // t),
        in_specs=[pl.BlockSpec((1, hp, t, LANES), lambda b, h, i: (b, h, i, 0)), whole, whole],
        out_specs=pl.BlockSpec((1, t, hp * MLA_V), lambda b, h, i: (b, i, h)),
        out_shape=jax.ShapeDtypeStruct((nb, ns, nh * MLA_V), BF16),
        compiler_params=_params(("arbitrary", "arbitrary", "arbitrary")), name="mla_flash",
    )(q, k, v)


def _masked_softmax(s, mask, axis):
    s = jnp.where(mask, s, NEG_INF)
    m = jnp.max(s, axis=axis, keepdims=True)
    e = jnp.where(mask, jnp.exp2(s - m), 0.0)
    return e / jnp.maximum(jnp.sum(e, axis=axis, keepdims=True), 1e-30)


def _softmax(s, axis):
    e = jnp.exp2(s - jnp.max(s, axis=axis, keepdims=True))
    return e / jnp.sum(e, axis=axis, keepdims=True)


def _top_blocks(imp, rounds):
    blk = lax.broadcasted_iota(jnp.int32, imp.shape, 0)
    sel = jnp.zeros(imp.shape, jnp.bool_)
    v = imp
    for _ in range(rounds):
        m = jnp.max(v, axis=0, keepdims=True)
        first = jnp.min(jnp.where(v == m, blk, imp.shape[0]), axis=0, keepdims=True)
        hit = blk == first
        sel = sel | hit
        v = jnp.where(hit, -jnp.inf, v)
    return sel


def _tile_heads(x, rep):
    return jnp.concatenate([x] * rep, axis=0)


def _nsa_prompt_kernel(q_ref, ka_ref, va_ref, win_ref, pool_ref, gate_ref, o_ref, *, tq, tk):
    qi = pl.program_id(1)
    nblk = pool_ref.shape[2]
    rep = NSA_REP
    rows = rep * tq
    start = qi * tq
    qpos4 = start + (lax.broadcasted_iota(jnp.int32, (rows, 1), 0) & (tq - 1))
    blk4 = lax.broadcasted_iota(jnp.int32, (rows, nblk), 1)
    blk_t = lax.broadcasted_iota(jnp.int32, (nblk, tq), 0)
    cur_t = (start + lax.broadcasted_iota(jnp.int32, (nblk, tq), 1)) // CMP_BLOCK
    blk_t4 = lax.broadcasted_iota(jnp.int32, (nblk, rows), 0)
    qpos_t4 = start + (lax.broadcasted_iota(jnp.int32, (nblk, rows), 1) & (tq - 1))
    forced = (blk_t == 0) | (blk_t == cur_t) | (blk_t == cur_t - 1)
    seen = blk_t <= cur_t
    tri = (lax.broadcasted_iota(jnp.int32, (rows, tq), 1)
           <= (lax.broadcasted_iota(jnp.int32, (rows, tq), 0) & (tq - 1)))
    gates = gate_ref[0]

    q_aug, o_cmp, o_win = [], [], []
    for g in range(NSA_GROUPS):
        q = q_ref[0, g * rep:(g + 1) * rep].reshape(rows, LANES)
        pool = pool_ref[0, g].astype(BF16)
        p_c = _masked_softmax(_dot_nt(q, pool), (blk4 + 1) * CMP_BLOCK - 1 <= qpos4, axis=1)
        o_cmp.append(_dot(p_c.astype(BF16), pool))
        p_t = _masked_softmax(_dot_nt(pool, q), (blk_t4 + 1) * CMP_BLOCK - 1 <= qpos_t4, axis=0)
        imp = p_t[:, 0:tq]
        for r in range(1, rep):
            imp = imp + p_t[:, r * tq:(r + 1) * tq]
        imp = jnp.where(forced | ~seen, -jnp.inf, imp)
        sel = (forced | _top_blocks(imp, SEL_TOPK - 3)) & seen
        bias = jnp.where(sel, 0.0, NEG_INF).T
        q_aug.append(jnp.concatenate([q, _tile_heads(bias, rep).astype(BF16)], axis=1))
        ntile = WINDOW // tq + 1
        scores, tiles = [], []
        for c in range(ntile):
            idx = qi - (ntile - 1) + c
            sl = pl.ds(pl.multiple_of(jnp.maximum(idx, 0) * tq, tq), tq)
            kvw = win_ref[0, g, sl, :]
            s = _dot_nt(q, kvw)
            if c == ntile - 1:
                s = jnp.where(tri, s, NEG_INF)
            else:
                if c == 0:
                    s = jnp.where(tri, NEG_INF, s)
                s = s + jnp.where(idx >= 0, 0.0, NEG_INF)
            scores.append(s)
            tiles.append(kvw)
        s_w = jnp.concatenate(scores, axis=1)
        e_w = jnp.exp2(s_w - jnp.max(s_w, axis=1, keepdims=True))
        l_w = jnp.sum(e_w, axis=1, keepdims=True)
        e_w = e_w.astype(BF16)
        o_w = _dot(e_w[:, 0:tq], tiles[0])
        for c in range(1, ntile):
            o_w = o_w + _dot(e_w[:, c * tq:(c + 1) * tq], tiles[c])
        o_win.append(o_w / l_w)

    def sel_step(j, carry, causal):
        sl = pl.ds(pl.multiple_of(j * tk, tk), tk)
        out = []
        for g in range(NSA_GROUPS):
            s = _dot_nt(q_aug[g], ka_ref[0, g, sl, :])
            if causal:
                kpos = j * tk + lax.broadcasted_iota(jnp.int32, (1, tk), 1)
                s = jnp.where(kpos <= qpos4, s, NEG_INF)
            out.append(_flash_update(*carry[g], s, va_ref[0, g, sl, :]))
        return tuple(out)

    jd = start // tk
    init = tuple((jnp.full((rows, 1), NEG_INF, F32), jnp.zeros((rows, LANES), F32)) for _ in range(NSA_GROUPS))
    carry = lax.fori_loop(0, jd, lambda j, c: sel_step(j, c, False), init)
    carry = sel_step(jd, carry, True)

    outs = []
    for g in range(NSA_GROUPS):
        o_s = _flash_finish(carry[g][1], NSA_DIM)
        for r in range(rep):
            hh = g * rep + r
            rs = slice(r * tq, (r + 1) * tq)
            outs.append(gates[:, 3 * hh:3 * hh + 1] * o_cmp[g][rs, NSA_DIM:]
                        + gates[:, 3 * hh + 1:3 * hh + 2] * o_s[rs]
                        + gates[:, 3 * hh + 2:3 * hh + 3] * o_win[g][rs, NSA_DIM:])
    o_ref[0] = jnp.concatenate(outs, axis=1).astype(o_ref.dtype)


def _nsa_prompt(q, ka, va, win, pool, gates, tq, tk):
    nb, nh, ns, _ = q.shape
    nblk = ns // CMP_BLOCK
    assert WINDOW % tq == 0 and tk % tq == 0
    kern = functools.partial(_nsa_prompt_kernel, tq=tq, tk=tk)
    whole = lambda a: pl.BlockSpec((1,) + a.shape[1:], lambda b, i: (b, 0, 0, 0))
    return pl.pallas_call(
        kern, grid=(nb, ns // tq),
        in_specs=[pl.BlockSpec((1, nh, tq, LANES), lambda b, i: (b, 0, i, 0)),
                  whole(ka), whole(va), whole(win),
                  pl.BlockSpec((1, NSA_GROUPS, nblk, LANES), lambda b, i: (b, 0, 0, 0)),
                  pl.BlockSpec((1, tq, LANES), lambda b, i: (b, i, 0))],
        out_specs=pl.BlockSpec((1, tq, nh * NSA_DIM), lambda b, i: (b, i, 0)),
        out_shape=jax.ShapeDtypeStruct((nb, ns, nh * NSA_DIM), BF16),
        compiler_params=_params(("arbitrary", "arbitrary")), name="nsa_prompt",
    )(q, ka, va, win, pool, gates)


def _head_matmul_kernel(a_ref, b_ref, o_ref):
    o_ref[0] = _dot(a_ref[0], b_ref[0]).astype(o_ref.dtype)


def _head_matmul(a, b, dtype):
    nh, m, k = a.shape
    n = b.shape[2]
    return pl.pallas_call(
        _head_matmul_kernel, grid=(nh,),
        in_specs=[pl.BlockSpec((1, m, k), lambda h: (h, 0, 0)), pl.BlockSpec((1, k, n), lambda h: (h, 0, 0))],
        out_specs=pl.BlockSpec((1, m, n), lambda h: (h, 0, 0)),
        out_shape=jax.ShapeDtypeStruct((nh, m, n), dtype),
        compiler_params=_params(("arbitrary",)), name="head_matmul",
    )(a, b)


def _new_token_mask(rows, t_new):
    tok = lax.broadcasted_iota(jnp.int32, (rows, LANES), 0) % t_new
    lane = lax.broadcasted_iota(jnp.int32, (rows, LANES), 1)
    return (lane < t_new) & (lane <= tok)


def _page_specs(pages, feat):
    return [pl.BlockSpec((1, feat, PAGE_SIZE), lambda b, j, pt, p=p: (pt[b, j * pages + p], 0, 0))
            for p in range(pages)]


def _mla_sample_kernel(pt_ref, q_ref, new_ref, *rest, pages, t_new):
    page_refs, (o_ref, kv_scr, new_scr, m_scr, l_scr, acc_scr) = rest[:pages], rest[pages:]
    j = pl.program_id(1)
    width = kv_scr.shape[0]

    @pl.when((pl.program_id(0) == 0) & (j == 0))
    def _():
        kv_scr[...] = jnp.zeros(kv_scr.shape, BF16)

    @pl.when(j == 0)
    def _():
        m0, l0, a0 = _flash_init(q_ref.shape[1], width)
        m_scr[...], l_scr[...], acc_scr[...] = m0, l0, a0

    for p in range(pages):
        kv_scr[0:MLA_ROW, p * PAGE_SIZE:(p + 1) * PAGE_SIZE] = page_refs[p][0].astype(BF16)
    q = q_ref[0]
    kv = kv_scr[...]
    carry = _online_update((m_scr[...], l_scr[...], acc_scr[...]), _dot(q, kv), lambda p: _dot_nt(p, kv))
    m_scr[...], l_scr[...], acc_scr[...] = carry

    @pl.when(j == pl.num_programs(1) - 1)
    def _():
        new_scr[...] = jnp.zeros(new_scr.shape, BF16)
        new_scr[0:t_new, 0:MLA_ROW] = new_ref[0].astype(BF16)
        kvn = new_scr[...]
        s = jnp.where(_new_token_mask(q.shape[0], t_new), _dot_nt(q, kvn), NEG_INF)
        _, l, acc = _online_update(carry, s, lambda p: _dot(p, kvn))
        o_ref[0] = acc / l


def _mla_sample(page_table, q, new_rows, cache_t, pages):
    nb, rows, width = q.shape
    t_new = new_rows.shape[1]
    n_pages = page_table.shape[1]
    kern = functools.partial(_mla_sample_kernel, pages=pages, t_new=t_new)
    grid_spec = pltpu.PrefetchScalarGridSpec(
        num_scalar_prefetch=1, grid=(nb, n_pages // pages),
        in_specs=[pl.BlockSpec((1, rows, width), lambda b, j, pt: (b, 0, 0)),
                  pl.BlockSpec((1, t_new, MLA_ROW), lambda b, j, pt: (b, 0, 0))] + _page_specs(pages, MLA_ROW),
        out_specs=pl.BlockSpec((1, rows, width), lambda b, j, pt: (b, 0, 0)),
        scratch_shapes=[pltpu.VMEM((width, pages * PAGE_SIZE), BF16), pltpu.VMEM((LANES, width), BF16),
                        pltpu.VMEM((rows, 1), F32), pltpu.VMEM((rows, 1), F32), pltpu.VMEM((rows, width), F32)])
    return pl.pallas_call(
        kern, grid_spec=grid_spec, out_shape=jax.ShapeDtypeStruct((nb, rows, width), F32),
        compiler_params=_params(("arbitrary", "arbitrary")), name="mla_sample",
    )(page_table, q, new_rows, *([cache_t] * pages))


def _cmp_pool_kernel(pt_ref, w_ref, e_ref, *rest, pages):
    page_refs, (o_ref, hi_scr, lo_scr) = rest[:pages], rest[pages:]
    w = w_ref[...]
    for p in range(pages):
        x = page_refs[p][0] * w
        hi = x.astype(BF16)
        sl = slice(p * PAGE_SIZE, (p + 1) * PAGE_SIZE)
        hi_scr[:, sl] = hi
        lo_scr[:, sl] = (x - hi.astype(F32)).astype(BF16)
    e = e_ref[...]
    o_ref[0] = _dot_nt(e, hi_scr[...]) + _dot_nt(e, lo_scr[...])


def _cmp_pool(page_table, w_pool_t, cache_t, pages):
    nb, n_pages = page_table.shape
    feat = cache_t.shape[1]
    per_step = pages * PAGE_SIZE // CMP_BLOCK
    expand = _block_expand(per_step, pages * PAGE_SIZE)
    kern = functools.partial(_cmp_pool_kernel, pages=pages)
    grid_spec = pltpu.PrefetchScalarGridSpec(
        num_scalar_prefetch=1, grid=(nb, n_pages // pages),
        in_specs=[pl.BlockSpec(w_pool_t.shape, lambda b, j, pt: (0, 0)),
                  pl.BlockSpec(expand.shape, lambda b, j, pt: (0, 0))] + _page_specs(pages, feat),
        out_specs=pl.BlockSpec((1, per_step, feat), lambda b, j, pt: (b, j, 0)),
        scratch_shapes=[pltpu.VMEM((feat, pages * PAGE_SIZE), BF16), pltpu.VMEM((feat, pages * PAGE_SIZE), BF16)])
    return pl.pallas_call(
        kern, grid_spec=grid_spec,
        out_shape=jax.ShapeDtypeStruct((nb, n_pages * PAGE_SIZE // CMP_BLOCK, feat), F32),
        compiler_params=_params(("arbitrary", "arbitrary")), name="cmp_pool",
    )(page_table, w_pool_t, expand, *([cache_t] * pages))


def _pack_group(rows, g):
    ko, vo = g * NSA_DIM, NSA_GROUPS * NSA_DIM + g * NSA_DIM
    return jnp.concatenate([rows[:, ko:ko + NSA_DIM], rows[:, vo:vo + NSA_DIM]], axis=1).astype(BF16)


def _pack_group_t(feats, g):
    ko, vo = g * NSA_DIM, NSA_GROUPS * NSA_DIM + g * NSA_DIM
    return jnp.concatenate([feats[ko:ko + NSA_DIM], feats[vo:vo + NSA_DIM]], axis=0).astype(BF16)


def _nsa_sample_kernel(pt_ref, q_ref, pool_ref, gate_ref, win_ref, wnew_ref, snew_ref, expand_ref, *rest,
                       pages, t_new):
    page_refs = rest[:pages]
    o_ref, kv_scr, new_scr, q_scr, sel_scr, oc_scr, ow_scr, m_scr, l_scr, acc_scr = rest[pages:]
    j = pl.program_id(1)
    rep = NSA_REP
    rows = rep * t_new
    nblk = pool_ref.shape[1]
    tk = pages * PAGE_SIZE
    new_ok = _new_token_mask(rows, t_new)

    def new_tile(ref, g):
        new_scr[...] = jnp.zeros(new_scr.shape, BF16)
        new_scr[0:t_new, :] = _pack_group(ref[0], g)
        return new_scr[...]

    @pl.when(j == 0)
    def _():
        pool = pool_ref[0]
        win_t = win_ref[0]
        wb = win_t.shape[1]
        tok = lax.broadcasted_iota(jnp.int32, (rows, 1), 0) % t_new
        lane = lax.broadcasted_iota(jnp.int32, (nblk, LANES), 1)
        blk = lax.broadcasted_iota(jnp.int32, (nblk, LANES), 0)
        for g in range(NSA_GROUPS):
            q = q_ref[0, g * rows:(g + 1) * rows]
            pg = _pack_group(pool, g)
            oc_scr[g] = _dot(_softmax(_dot_nt(q, pg), axis=1).astype(BF16), pg)
            q_scr[...] = jnp.zeros(q_scr.shape, BF16)
            q_scr[0:rows, :] = q
            p_t = _softmax(_dot_nt(pg, q_scr[...]), axis=0)
            tot = p_t
            for r in range(1, rep):
                tot = tot + pltpu.roll(p_t, LANES - r * t_new, 1)
            tot = jnp.where(lane < t_new, tot, 0.0)
            imp = tot
            for r in range(1, rep):
                imp = imp + pltpu.roll(tot, r * t_new, 1)
            forced = (blk == 0) | (blk == nblk - 1)
            sel = forced | _top_blocks(jnp.where(forced, -jnp.inf, imp), SEL_TOPK - 3)
            sel_scr[g] = jnp.where(sel, 1.0, 0.0).T[0:rows]
            wg = _pack_group_t(win_t, g)
            i = lax.broadcasted_iota(jnp.int32, (1, wb), 1)
            ng = new_tile(wnew_ref, g)
            s_w = jnp.concatenate([jnp.where(i > tok + (wb - WINDOW), _dot(q, wg), NEG_INF),
                                   jnp.where(new_ok, _dot_nt(q, ng), NEG_INF)], axis=1)
            p_w = _softmax(s_w, axis=1).astype(BF16)
            ow_scr[g] = _dot_nt(p_w[:, :wb], wg) + _dot(p_w[:, wb:], ng)
            m0, l0, a0 = _flash_init(rows, LANES)
            m_scr[g], l_scr[g], acc_scr[g] = m0, l0, a0

    for p in range(pages):
        page = page_refs[p][0]
        for g in range(NSA_GROUPS):
            kv_scr[g, :, p * PAGE_SIZE:(p + 1) * PAGE_SIZE] = _pack_group_t(page, g)
    sl = pl.ds(pl.multiple_of(j * tk, tk), tk)
    for g in range(NSA_GROUPS):
        q = q_ref[0, g * rows:(g + 1) * rows]
        kv = kv_scr[g]
        hit = _dot(sel_scr[g].astype(BF16), expand_ref[:, sl])
        s = _dot(q, kv) + (hit - 1.0) * (-NEG_INF)
        carry = _online_update((m_scr[g], l_scr[g], acc_scr[g]), s, lambda p, kv=kv: _dot_nt(p, kv))
        m_scr[g], l_scr[g], acc_scr[g] = carry

    @pl.when(j == pl.num_programs(1) - 1)
    def _():
        gates = gate_ref[0]
        for g in range(NSA_GROUPS):
            q = q_ref[0, g * rows:(g + 1) * rows]
            ng = new_tile(snew_ref, g)
            s = jnp.where(new_ok, _dot_nt(q, ng), NEG_INF)
            _, l, acc = _online_update((m_scr[g], l_scr[g], acc_scr[g]), s, lambda p, ng=ng: _dot(p, ng))
            gg = gates[g * rows:(g + 1) * rows]
            o_ref[0, g * rows:(g + 1) * rows, :] = (gg[:, 0:1] * oc_scr[g] + gg[:, 1:2] * (acc / l)
                                                    + gg[:, 2:3] * ow_scr[g])


def _nsa_sample(page_table, q, pool, gates, win_t, win_new, slc_new, expand, cache_t, pages):
    nb, rows2, _ = q.shape
    t_new = slc_new.shape[1]
    n_pages = page_table.shape[1]
    feat = cache_t.shape[1]
    rows = rows2 // NSA_GROUPS
    per_b = lambda shape: pl.BlockSpec((1,) + shape, lambda b, j, pt: (b, 0, 0))
    kern = functools.partial(_nsa_sample_kernel, pages=pages, t_new=t_new)
    grp = lambda *shape: pltpu.VMEM((NSA_GROUPS,) + shape, F32)
    grid_spec = pltpu.PrefetchScalarGridSpec(
        num_scalar_prefetch=1, grid=(nb, n_pages // pages),
        in_specs=[per_b((rows2, LANES)), per_b(pool.shape[1:]), per_b((rows2, LANES)), per_b(win_t.shape[1:]),
                  per_b((t_new, feat)), per_b((t_new, feat)),
                  pl.BlockSpec(expand.shape, lambda b, j, pt: (0, 0))] + _page_specs(pages, feat),
        out_specs=per_b((rows2, LANES)),
        scratch_shapes=[pltpu.VMEM((NSA_GROUPS, LANES, pages * PAGE_SIZE), BF16), pltpu.VMEM((LANES, LANES), BF16),
                        pltpu.VMEM((LANES, LANES), BF16), grp(rows, expand.shape[0]),
                        grp(rows, LANES), grp(rows, LANES), grp(rows, 1), grp(rows, 1), grp(rows, LANES)])
    return pl.pallas_call(
        kern, grid_spec=grid_spec, out_shape=jax.ShapeDtypeStruct((nb, rows2, LANES), F32),
        compiler_params=_params(("arbitrary", "arbitrary")), name="nsa_sample",
    )(page_table, q, pool, gates, win_t, win_new, slc_new, expand, *([cache_t] * pages))


def _tail_kernel(x_ref, ya_ref, yb_ref, ga_ref, gb_ref, p_ref, wba_ref, wbn_ref, wo_ref, lnf_ref,
                 wg_ref, wu_ref, wd_ref, lnp_ref, wpg_ref, wpp_ref, lnl_ref, o_ref, hid_scr, *, chunk):
    merged = ga_ref[...] * _dot(ya_ref[...], wba_ref[...]) + gb_ref[...] * _dot(yb_ref[...], wbn_ref[...])
    x = x_ref[...] + _dot(merged.astype(BF16), wo_ref[...])
    h = _rms(x, lnf_ref[...]).astype(BF16)

    def ffn(c, carry):
        sl = pl.ds(pl.multiple_of(c * chunk, chunk), chunk)
        hid_scr[:, sl] = (jax.nn.silu(_dot(h, wg_ref[:, sl])) * _dot(h, wu_ref[:, sl])).astype(BF16)
        return carry

    lax.fori_loop(0, wg_ref.shape[1] // chunk, ffn, 0)
    x = x + _dot(hid_scr[...], wd_ref[...])
    gate = jax.nn.sigmoid(_dot(_rms(x, lnp_ref[...]).astype(BF16), wpg_ref[...]))
    x = x + gate * _dot(p_ref[...].astype(BF16), wpp_ref[...])
    o_ref[...] = _rms(x, lnl_ref[...])


def _tail(x, ya, yb, ga, gb, p, weights, tm):
    n, d_model = x.shape
    row = lambda a: pl.BlockSpec((tm, a.shape[1]), lambda i: (i, 0))
    acts = (x, ya, yb, ga, gb, p)
    kern = functools.partial(_tail_kernel, chunk=2 * LANES)
    return pl.pallas_call(
        kern, grid=(n // tm,),
        in_specs=[row(a) for a in acts] + [_resident(w.shape) for w in weights],
        out_specs=pl.BlockSpec((tm, d_model), lambda i: (i, 0)),
        out_shape=jax.ShapeDtypeStruct((n, d_model), F32),
        scratch_shapes=[pltpu.VMEM((tm, weights[4].shape[1]), BF16)],
        compiler_params=_params(("arbitrary",)), name="tail",
    )(*acts, *weights)


def _rope_tables(pos):
    pos = pos.astype(F32)[:, None]
    n = pos.shape[0]

    def cs(rot):
        half = rot // 2
        inv = jnp.float32(ROPE_THETA) ** (-jnp.arange(half, dtype=F32) * 2.0 / rot)
        ang = pos * inv[None, :]
        return jnp.cos(ang), jnp.sin(ang)

    one = lambda w: jnp.ones((n, w), F32)
    zero = lambda w: jnp.zeros((n, w), F32)
    c, s = cs(MLA_ROPE)
    pad = LANES - MLA_NOPE - MLA_ROPE
    cm = jnp.concatenate([one(MLA_NOPE), c, c, one(pad)], axis=1)
    sm = jnp.concatenate([zero(MLA_NOPE), -s, s, zero(pad)], axis=1)
    c, s = cs(NSA_ROT)
    rest = NSA_DIM - NSA_ROT
    cn = jnp.concatenate([c, c, one(rest)] * (LANES // NSA_DIM), axis=1)
    sn = jnp.concatenate([-s, s, zero(rest)] * (LANES // NSA_DIM), axis=1)
    return cm, sm, cn, sn


def _layout_w_in(w_in):
    d_model = w_in.shape[0]
    sizes = (MLA_HEADS * (MLA_NOPE + MLA_ROPE), MLA_KV_LORA, MLA_ROPE, NSA_HEADS * NSA_DIM,
             SEG_KV // 3, SEG_KV // 3, SEG_KV // 3, 3 * NSA_HEADS, 2 * d_model)
    offs = np.concatenate([[0], np.cumsum(sizes)])
    qa, ca, ra, qb, zc, zs, zw, gn, gm = [w_in[:, int(offs[i]):int(offs[i + 1])] for i in range(len(sizes))]
    qa = qa.reshape(d_model, MLA_HEADS, MLA_NOPE + MLA_ROPE)
    qa = jnp.pad(qa, ((0, 0), (0, 0), (0, LANES - MLA_NOPE - MLA_ROPE))).reshape(d_model, SEG_QA)
    ra = jnp.pad(ra, ((0, 0), (MLA_NOPE, LANES - MLA_NOPE - MLA_ROPE)))
    gn = jnp.pad(gn, ((0, 0), (0, SEG_GN - gn.shape[1])))
    return jnp.concatenate([qa, ca, ra, qb, zc, zs, zw, gn, gm], axis=1).astype(BF16)


def _layout_w_up(w_uk, w_uv):
    wk = jnp.pad(w_uk, ((0, 0), (0, 0), (0, LANES - MLA_NOPE))).reshape(MLA_KV_LORA, SEG_QA)
    return jnp.concatenate([wk, w_uv.reshape(MLA_KV_LORA, MLA_HEADS * MLA_V)], axis=1).astype(BF16)


def _layout_absorb(w_uk, w_uv, width):
    to_lat = jnp.zeros((MLA_HEADS, LANES, width), F32)
    to_lat = to_lat.at[:, 0:MLA_NOPE, 0:MLA_KV_LORA].set(w_uk.transpose(1, 2, 0))
    to_lat = to_lat.at[:, MLA_NOPE:MLA_NOPE + MLA_ROPE, MLA_KV_LORA:MLA_ROW].set(jnp.eye(MLA_ROPE, dtype=F32))
    to_val = jnp.zeros((MLA_HEADS, width, MLA_V), F32).at[:, 0:MLA_KV_LORA, :].set(w_uv.transpose(1, 0, 2))
    return to_lat.astype(BF16), to_val.astype(BF16)


def _block_expand(nblk, nkeys):
    return (jnp.arange(nkeys)[None, :] // CMP_BLOCK == jnp.arange(nblk)[:, None]).astype(BF16)


def kernel(x_prompt, x_sample, cache_mla, cache_nsa_cmp, cache_nsa_slc, state_nsa_win, page_table, p_prompt, p_sample, ln_attn, w_in, mla_kv_norm, mla_w_uk, mla_w_uv, nsa_w_cmp_k, nsa_w_cmp_v, w_branch_mla, w_branch_nsa, w_out, ln_ffn, w_ffn_gate, w_ffn_up, w_ffn_down, ln_ple, w_ple_gate, w_ple_proj, ln_final):
    nb, ns, d_model = x_prompt.shape
    db, t_new, _ = x_sample.shape
    depth = w_in.shape[0]
    n_pages = page_table.shape[1]
    past = n_pages * PAGE_SIZE
    wb = state_nsa_win.shape[2]
    width = 2 * NSA_GROUPS * NSA_DIM
    g_, d_ = NSA_GROUPS, NSA_DIM
    assert depth == 1 and t_new <= CMP_BLOCK and past % CMP_BLOCK == 0 and wb == WINDOW and past >= wb
    assert ns % 512 == 0 and (db * t_new) % 512 == 0 and NSA_REP * t_new <= LANES
    tm = 512
    pages = min(64, n_pages)
    lat_w = 3 * LANES
    i = 0

    w_all = _layout_w_in(w_in[i])
    w_up = _layout_w_up(mla_w_uk[i], mla_w_uv[i])
    to_lat, to_val = _layout_absorb(mla_w_uk[i], mla_w_uv[i], lat_w)
    w_pool = jnp.concatenate([nsa_w_cmp_k[i]] * g_ + [nsa_w_cmp_v[i]] * g_, axis=1)
    w_pool_t = jnp.tile(w_pool.T, (1, PAGE_SIZE // CMP_BLOCK))
    ln_a = ln_attn[i][None, :]
    kvn = mla_kv_norm[i][None, :]
    tail_w = (w_branch_mla[i].astype(BF16), w_branch_nsa[i].astype(BF16), w_out[i].astype(BF16),
              ln_ffn[i][None, :], w_ffn_gate[i].astype(BF16), w_ffn_up[i].astype(BF16),
              w_ffn_down[i].astype(BF16), ln_ple[i][None, :], w_ple_gate[i].astype(BF16),
              w_ple_proj[i].astype(BF16), ln_final[None, :])

    xp = x_prompt.reshape(nb * ns, d_model)
    nblk = ns // CMP_BLOCK
    (qm, km, vm, mla_p, qn, kvc_p, kvs_p, kvw_p, ka, va, win, pool, gn, ga, gb) = _proj(
        xp, _rope_tables(jnp.arange(ns)), ln_a, w_all, w_up, kvn, w_pool, nb, ns, tm, nblk)
    token_major = lambda a: a.transpose(0, 2, 1).reshape(-1, a.shape[1])
    mla_p, kvc_p, kvs_p, kvw_p = (token_major(a) for a in (mla_p, kvc_p, kvs_p, kvw_p))
    ya = _mla_flash(qm, km, vm, 512)
    pool5 = pool.reshape(nb, nblk, 2, g_, d_)
    pool_g = jnp.concatenate([pool5[:, :, 0].transpose(0, 2, 1, 3), pool5[:, :, 1].transpose(0, 2, 1, 3)], axis=-1)
    yb = _nsa_prompt(qn, ka, va, win, pool_g, gn.reshape(nb, ns, LANES), 128, 512)
    y_prompt = _tail(xp, ya.reshape(nb * ns, -1), yb.reshape(nb * ns, -1), ga, gb,
                     p_prompt[i].reshape(nb * ns, -1), tail_w, tm).reshape(nb, ns, d_model)

    n_s = db * t_new
    xs = x_sample.reshape(n_s, d_model)
    pos_s = jnp.tile(past + jnp.arange(t_new), db)
    (qm, _, _, mla_s, qn, kvc_s, kvs_s, kvw_s, _, _, _, _, gn, ga, gb) = _proj(
        xs, _rope_tables(pos_s), ln_a, w_all, w_up, kvn, w_pool, 1, n_s, tm, LANES)
    mla_s, kvc_s, kvs_s, kvw_s = (token_major(a) for a in (mla_s, kvc_s, kvs_s, kvw_s))
    by_batch = lambda a: a.reshape(a.shape[0], db, t_new, a.shape[-1]).transpose(1, 0, 2, 3).reshape(
        db, a.shape[0] * t_new, a.shape[-1])
    q_lat = by_batch(_head_matmul(qm[0], to_lat, BF16))
    mla_t = cache_mla[i].transpose(0, 2, 1)
    o_lat = _mla_sample(page_table, q_lat, mla_s.reshape(db, t_new, MLA_ROW), mla_t, pages)
    o_lat = o_lat.reshape(db, MLA_HEADS, t_new, lat_w).transpose(1, 0, 2, 3).reshape(MLA_HEADS, n_s, lat_w)
    ya = _head_matmul(o_lat.astype(BF16), to_val, BF16).transpose(1, 0, 2).reshape(n_s, MLA_HEADS * MLA_V)

    cmp_t = cache_nsa_cmp[i].reshape(-1, PAGE_SIZE, width).transpose(0, 2, 1)
    slc_t = cache_nsa_slc[i].reshape(-1, PAGE_SIZE, width).transpose(0, 2, 1)
    pool_s = _cmp_pool(page_table, w_pool_t, cmp_t, pages)
    gate_s = gn[:, :3 * NSA_HEADS].reshape(db, t_new, NSA_HEADS, 3).transpose(0, 2, 1, 3).reshape(
        db, NSA_HEADS * t_new, 3)
    gate_s = jnp.pad(gate_s, ((0, 0), (0, 0), (0, LANES - 3)))
    win_state = state_nsa_win[i].reshape(db, wb, width)
    o_nsa = _nsa_sample(page_table, by_batch(qn[0]), pool_s, gate_s, win_state.transpose(0, 2, 1),
                        kvw_s.reshape(db, t_new, width), kvs_s.reshape(db, t_new, width),
                        _block_expand(past // CMP_BLOCK, past), slc_t, pages)
    yb = o_nsa[:, :, NSA_DIM:].reshape(db, NSA_HEADS, t_new, d_).transpose(0, 2, 1, 3).reshape(n_s, -1)
    y_sample = _tail(xs, ya, yb.astype(BF16), ga, gb, p_sample[i].reshape(n_s, -1), tail_w, tm).reshape(
        db, t_new, d_model)

    kv5 = lambda a, b, s: a.reshape(1, b, s, 2, g_, d_)
    win_all = jnp.concatenate([win_state, kvw_s.reshape(db, t_new, width)], axis=1)
    wkeep = min(WINDOW, ns)
    return (y_prompt, y_sample,
            mla_p.reshape(1, nb, ns, MLA_ROW), mla_s.reshape(1, db, t_new, MLA_ROW),
            kv5(kvc_p, nb, ns), kv5(kvc_s, db, t_new), kv5(kvs_p, nb, ns), kv5(kvs_s, db, t_new),
            kv5(kvw_p, nb, ns)[:, :, ns - wkeep:],
            kv5(win_all[:, -min(WINDOW, past + t_new):], db, min(WINDOW, past + t_new)))
```

```python
import functools
import math

import jax
import jax.numpy as jnp
import numpy as np
from jax import lax
from jax.experimental import pallas as pl
from jax.experimental.pallas import tpu as pltpu

F32 = jnp.float32
BF16 = jnp.bfloat16

NORM_EPS = 1e-6
ROPE_THETA = 500000.0
NEG_INF = -1e30
LOG2E = math.log2(math.e)
PAGE_SIZE = 128
MLA_HEADS = 8
MLA_NOPE = 64
MLA_ROPE = 32
MLA_V = 64
MLA_KV_LORA = 256
MLA_ROW = MLA_KV_LORA + MLA_ROPE
NSA_HEADS = 8
NSA_GROUPS = 2
NSA_REP = NSA_HEADS // NSA_GROUPS
NSA_DIM = 64
NSA_ROT = NSA_DIM // 4
CMP_BLOCK = 64
SEL_TOPK = 16
WINDOW = 512
LANES = 128
VMEM_LIMIT = 60 * 1024 * 1024

SEG_QA = MLA_HEADS * LANES
SEG_C = MLA_KV_LORA
SEG_R = LANES
SEG_QB = NSA_HEADS * NSA_DIM
SEG_KV = 3 * 2 * NSA_GROUPS * NSA_DIM
SEG_GN = LANES
OFF_C = SEG_QA
OFF_R = OFF_C + SEG_C
OFF_QB = OFF_R + SEG_R
OFF_KV = OFF_QB + SEG_QB
OFF_GN = OFF_KV + SEG_KV
OFF_GM = OFF_GN + SEG_GN


def _params(sem):
    return pltpu.CompilerParams(dimension_semantics=sem, vmem_limit_bytes=VMEM_LIMIT)


def _resident(shape):
    nd = len(shape)
    return pl.BlockSpec(shape, lambda *_: (0,) * nd, pipeline_mode=pl.Buffered(1))


def _dot(a, b):
    return jnp.dot(a, b, preferred_element_type=F32)


def _dot_nt(a, b):
    return lax.dot_general(a, b, (((1,), (1,)), ((), ())), preferred_element_type=F32)


def _rms(x, g):
    return x * lax.rsqrt(jnp.mean(x * x, axis=-1, keepdims=True) + NORM_EPS) * g


def _value_slot(v, rows):
    lane = lax.broadcasted_iota(jnp.int32, (rows, LANES - v.shape[1]), 1)
    return jnp.concatenate([v, jnp.where(lane == 0, 1.0, 0.0)], axis=1).astype(BF16)


def _proj_kernel(x_ref, ln_ref, w_ref, wup_ref, kvn_ref, cm_ref, sm_ref, cn_ref, sn_ref, wpool_ref,
                 qm_ref, km_ref, vm_ref, mla_ref, qn_ref, kvc_ref, kvs_ref, kvw_ref,
                 ka_ref, va_ref, win_ref, pool_ref, gn_ref, ga_ref, gb_ref, *, nt):
    tm = x_ref.shape[0]
    d_model = x_ref.shape[1]
    nblk = ka_ref.shape[3] - LANES
    h = _rms(x_ref[...], ln_ref[...]).astype(BF16)
    lane = lax.broadcasted_iota(jnp.int32, (tm, LANES), 1)
    cm, sm, cn, sn = cm_ref[...], sm_ref[...], cn_ref[...], sn_ref[...]

    def rope_m(z):
        half = MLA_ROPE // 2
        sw = jnp.where(lane < MLA_NOPE + half, pltpu.roll(z, LANES - half, 1), pltpu.roll(z, half, 1))
        return z * cm + sw * sm

    def rope_n(z):
        half = NSA_ROT // 2
        sw = jnp.where((lane & (NSA_DIM - 1)) < half, pltpu.roll(z, LANES - half, 1), pltpu.roll(z, half, 1))
        return z * cn + sw * sn

    zq = _dot(h, w_ref[:, 0:SEG_QA])
    scale_m = (MLA_NOPE + MLA_ROPE) ** -0.5 * LOG2E
    for hh in range(MLA_HEADS):
        qm_ref[0, hh] = (rope_m(zq[:, hh * LANES:(hh + 1) * LANES]) * scale_m).astype(BF16)

    zc = _dot(h, w_ref[:, OFF_C:OFF_R])
    c = _rms(zc, kvn_ref[...])
    kpe = rope_m(_dot(h, w_ref[:, OFF_R:OFF_QB]))
    mla_ref[0, 0:MLA_KV_LORA, :] = c.T
    mla_ref[0, MLA_KV_LORA:MLA_ROW, :] = kpe.T[MLA_NOPE:MLA_NOPE + MLA_ROPE]
    up = _dot(c.astype(BF16), wup_ref[...])
    for hh in range(MLA_HEADS):
        km_ref[0, hh] = (up[:, hh * LANES:(hh + 1) * LANES] + kpe).astype(BF16)
        vo = SEG_QA + hh * MLA_V
        vm_ref[0, hh] = _value_slot(up[:, vo:vo + MLA_V], tm)

    zqb = _dot(h, w_ref[:, OFF_QB:OFF_KV])
    zero_half = jnp.zeros((tm, NSA_DIM), F32)
    scale_n = NSA_DIM ** -0.5 * LOG2E
    for pair in range(NSA_HEADS // 2):
        zz = rope_n(zqb[:, pair * LANES:(pair + 1) * LANES]) * scale_n
        qn_ref[0, 2 * pair] = jnp.concatenate([zz[:, :NSA_DIM], zero_half], axis=1).astype(BF16)
        qn_ref[0, 2 * pair + 1] = jnp.concatenate([zz[:, NSA_DIM:], zero_half], axis=1).astype(BF16)

    zkv = _dot(h, w_ref[:, OFF_KV:OFF_GN])
    width = 2 * NSA_GROUPS * NSA_DIM

    def kv_rows(i):
        z = zkv[:, i * width:(i + 1) * width]
        return jnp.concatenate([rope_n(z[:, :LANES]), z[:, LANES:]], axis=1)

    rows_c, rows_s, rows_w = kv_rows(0), kv_rows(1), kv_rows(2)
    kvc_ref[0] = rows_c.T
    kvs_ref[0] = rows_s.T
    kvw_ref[0] = rows_w.T
    pos = (pl.program_id(0) % nt) * tm + lax.broadcasted_iota(jnp.int32, (tm, nblk), 0)
    onehot = jnp.where(lax.broadcasted_iota(jnp.int32, (tm, nblk), 1) == pos // CMP_BLOCK, 1.0, 0.0)
    for g in range(NSA_GROUPS):
        ko, vo = g * NSA_DIM, LANES + g * NSA_DIM
        ka_ref[0, g] = jnp.concatenate([rows_s[:, ko:ko + NSA_DIM], zero_half, onehot], axis=1).astype(BF16)
        va_ref[0, g] = _value_slot(rows_s[:, vo:vo + NSA_DIM], tm)
        win_ref[0, g] = jnp.concatenate([rows_w[:, ko:ko + NSA_DIM], rows_w[:, vo:vo + NSA_DIM]],
                                        axis=1).astype(BF16)
    pool_ref[...] = jnp.sum(rows_c.reshape(tm // CMP_BLOCK, CMP_BLOCK, width) * wpool_ref[...][None], axis=1)

    gn_ref[...] = jax.nn.sigmoid(_dot(h, w_ref[:, OFF_GN:OFF_GM]))
    zm = _dot(h, w_ref[:, OFF_GM:OFF_GM + 2 * d_model])
    ga_ref[...] = jax.nn.sigmoid(zm[:, :d_model])
    gb_ref[...] = jax.nn.sigmoid(zm[:, d_model:])


def _proj(x, tabs, ln, w_all, w_up, kv_norm, w_pool, nb, ns, tm, nblk):
    n, d_model = x.shape
    nt = ns // tm
    width = 2 * NSA_GROUPS * NSA_DIM
    row = lambda w: pl.BlockSpec((tm, w), lambda i: (i, 0))
    head = lambda hn, w: pl.BlockSpec((1, hn, tm, w), lambda i: (i // nt, 0, i % nt, 0))
    tab = pl.BlockSpec((tm, LANES), lambda i: (i % nt, 0))
    feat = lambda f: pl.BlockSpec((1, f, tm), lambda i: (i // nt, 0, i % nt))
    slots = (
        (head(MLA_HEADS, LANES), (nb, MLA_HEADS, ns, LANES), BF16),
        (head(MLA_HEADS, LANES), (nb, MLA_HEADS, ns, LANES), BF16),
        (head(MLA_HEADS, LANES), (nb, MLA_HEADS, ns, LANES), BF16),
        (feat(MLA_ROW), (nb, MLA_ROW, ns), F32),
        (head(NSA_HEADS, LANES), (nb, NSA_HEADS, ns, LANES), BF16),
        (feat(width), (nb, width, ns), F32),
        (feat(width), (nb, width, ns), F32),
        (feat(width), (nb, width, ns), F32),
        (head(NSA_GROUPS, LANES + nblk), (nb, NSA_GROUPS, ns, LANES + nblk), BF16),
        (head(NSA_GROUPS, LANES), (nb, NSA_GROUPS, ns, LANES), BF16),
        (head(NSA_GROUPS, LANES), (nb, NSA_GROUPS, ns, LANES), BF16),
        (pl.BlockSpec((tm // CMP_BLOCK, width), lambda i: (i, 0)), (n // CMP_BLOCK, width), F32),
        (row(LANES), (n, LANES), F32),
        (row(d_model), (n, d_model), F32),
        (row(d_model), (n, d_model), F32),
    )
    in_specs = [row(d_model), _resident(ln.shape), _resident(w_all.shape), _resident(w_up.shape),
                _resident(kv_norm.shape), tab, tab, tab, tab, _resident(w_pool.shape)]
    return pl.pallas_call(
        functools.partial(_proj_kernel, nt=nt), grid=(n // tm,), in_specs=in_specs,
        out_specs=tuple(s[0] for s in slots),
        out_shape=tuple(jax.ShapeDtypeStruct(s[1], s[2]) for s in slots),
        compiler_params=_params(("arbitrary",)), name="proj",
    )(x, ln, w_all, w_up, kv_norm, *tabs, w_pool)


def _flash_update(m, acc, s, v):
    m_new = jnp.maximum(m, jnp.max(s, axis=-1, keepdims=True))
    alpha = jnp.exp2(m - m_new)
    p = jnp.exp2(s - m_new).astype(BF16)
    return m_new, alpha * acc + _dot(p, v)


def _flash_finish(acc, dv):
    return acc[:, :dv] / acc[:, dv:dv + 1]


def _online_update(carry, s, pv):
    m, l, acc = carry
    m_new = jnp.maximum(m, jnp.max(s, axis=-1, keepdims=True))
    alpha = jnp.exp2(m - m_new)
    p = jnp.exp2(s - m_new)
    l = alpha * l + jnp.sum(p, axis=-1, keepdims=True)
    acc = alpha * acc + pv(p.astype(BF16))
    return m_new, l, acc


def _flash_init(rows, dv):
    return (jnp.full((rows, 1), NEG_INF, F32), jnp.zeros((rows, 1), F32), jnp.zeros((rows, dv), F32))


def _mla_flash_kernel(q_ref, k_ref, v_ref, o_ref, *, t):
    qi = pl.program_id(2)
    heads = q_ref.shape[1]
    qs = [q_ref[0, hh] for hh in range(heads)]
    causal = lax.broadcasted_iota(jnp.int32, (t, t), 1) <= lax.broadcasted_iota(jnp.int32, (t, t), 0)

    def step(j, carry, mask):
        sl = pl.ds(pl.multiple_of(j * t, t), t)
        scores = [_dot_nt(qs[hh], k_ref[0, hh, sl, :]) for hh in range(heads)]
        if mask is not None:
            scores = [jnp.where(mask, s, NEG_INF) for s in scores]
        return tuple(_flash_update(*carry[hh], scores[hh], v_ref[0, hh, sl, :]) for hh in range(heads))

    init = tuple((jnp.full((t, 1), NEG_INF, F32), jnp.zeros((t, LANES), F32)) for _ in range(heads))
    carry = lax.fori_loop(0, qi, lambda j, c: step(j, c, None), init)
    carry = step(qi, carry, causal)
    o_ref[0] = jnp.concatenate([_flash_finish(acc, MLA_V) for _, acc in carry], axis=1).astype(o_ref.dtype)


def _mla_flash(q, k, v, t):
    nb, nh, ns, _ = q.shape
    hp = 4
    kern = functools.partial(_mla_flash_kernel, t=t)
    whole = pl.BlockSpec((1, hp, ns, LANES), lambda b, h, i: (b, h, 0, 0))
    return pl.pallas_call(
        kern, grid=(nb, nh // hp, ns // t),
        in_specs=[pl.BlockSpec((1, hp, t, LANES), lambda b, h, i: (b, h, i, 0)), whole, whole],
        out_specs=pl.BlockSpec((1, t, hp * MLA_V), lambda b, h, i: (b, i, h)),
        out_shape=jax.ShapeDtypeStruct((nb, ns, nh * MLA_V), BF16),
        compiler_params=_params(("arbitrary", "arbitrary", "arbitrary")), name="mla_flash",
    )(q, k, v)


def _masked_softmax(s, mask, axis):
    s = jnp.where(mask, s, NEG_INF)
    m = jnp.max(s, axis=axis, keepdims=True)
    e = jnp.where(mask, jnp.exp2(s - m), 0.0)
    return e / jnp.maximum(jnp.sum(e, axis=axis, keepdims=True), 1e-30)


def _softmax(s, axis):
    e = jnp.exp2(s - jnp.max(s, axis=axis, keepdims=True))
    return e / jnp.sum(e, axis=axis, keepdims=True)


def _top_blocks(imp, rounds):
    blk = lax.broadcasted_iota(jnp.int32, imp.shape, 0)
    sel = jnp.zeros(imp.shape, jnp.bool_)
    v = imp
    for _ in range(rounds):
        m = jnp.max(v, axis=0, keepdims=True)
        first = jnp.min(jnp.where(v == m, blk, imp.shape[0]), axis=0, keepdims=True)
        hit = blk == first
        sel = sel | hit
        v = jnp.where(hit, -jnp.inf, v)
    return sel


def _tile_heads(x, rep):
    return jnp.concatenate([x] * rep, axis=0)


def _nsa_prompt_kernel(q_ref, ka_ref, va_ref, win_ref, pool_ref, gate_ref, o_ref, *, tq, tk):
    qi = pl.program_id(1)
    nblk = pool_ref.shape[2]
    rep = NSA_REP
    rows = rep * tq
    start = qi * tq
    qpos4 = start + (lax.broadcasted_iota(jnp.int32, (rows, 1), 0) & (tq - 1))
    blk4 = lax.broadcasted_iota(jnp.int32, (rows, nblk), 1)
    blk_t = lax.broadcasted_iota(jnp.int32, (nblk, tq), 0)
    cur_t = (start + lax.broadcasted_iota(jnp.int32, (nblk, tq), 1)) // CMP_BLOCK
    blk_t4 = lax.broadcasted_iota(jnp.int32, (nblk, rows), 0)
    qpos_t4 = start + (lax.broadcasted_iota(jnp.int32, (nblk, rows), 1) & (tq - 1))
    forced = (blk_t == 0) | (blk_t == cur_t) | (blk_t == cur_t - 1)
    seen = blk_t <= cur_t
    own = (blk_t >= start // CMP_BLOCK) & (blk_t < (start + tq) // CMP_BLOCK)
    tri = (lax.broadcasted_iota(jnp.int32, (rows, tq), 1)
           <= (lax.broadcasted_iota(jnp.int32, (rows, tq), 0) & (tq - 1)))
    gates = gate_ref[0]

    ntile = WINDOW // tq + 1
    qs, pools, s_cmp, s_blk, s_win, win_tiles = [], [], [], [], [], []
    for g in range(NSA_GROUPS):
        q = q_ref[0, g * rep:(g + 1) * rep].reshape(rows, LANES)
        pool = pool_ref[0, g].astype(BF16)
        qs.append(q)
        pools.append(pool)
        s_cmp.append(_dot_nt(q, pool))
        s_blk.append(_dot_nt(pool, q))
        scores, tiles = [], []
        for c in range(ntile):
            idx = qi - (ntile - 1) + c
            sl = pl.ds(pl.multiple_of(jnp.maximum(idx, 0) * tq, tq), tq)
            kvw = win_ref[0, g, sl, :]
            s = _dot_nt(q, kvw)
            if c == ntile - 1:
                s = jnp.where(tri, s, NEG_INF)
            else:
                if c == 0:
                    s = jnp.where(tri, NEG_INF, s)
                s = s + jnp.where(idx >= 0, 0.0, NEG_INF)
            scores.append(s)
            tiles.append(kvw)
        s_win.append(jnp.concatenate(scores, axis=1))
        win_tiles.append(tiles)

    q_aug, o_cmp, o_win = [], [], []
    for g in range(NSA_GROUPS):
        q, pool, tiles = qs[g], pools[g], win_tiles[g]
        p_c = _masked_softmax(s_cmp[g], (blk4 + 1) * CMP_BLOCK - 1 <= qpos4, axis=1)
        o_cmp.append(_dot(p_c.astype(BF16), pool))
        p_t = _masked_softmax(s_blk[g], (blk_t4 + 1) * CMP_BLOCK - 1 <= qpos_t4, axis=0)
        imp = p_t[:, 0:tq]
        for r in range(1, rep):
            imp = imp + p_t[:, r * tq:(r + 1) * tq]
        imp = jnp.where(forced | ~seen, -jnp.inf, imp)
        sel = (forced | _top_blocks(imp, SEL_TOPK - 3)) & seen
        bias = jnp.where(sel & ~own, 0.0, NEG_INF).T
        q_aug.append(jnp.concatenate([q, _tile_heads(bias, rep).astype(BF16)], axis=1))
        s_w = s_win[g]
        e_w = jnp.exp2(s_w - jnp.max(s_w, axis=1, keepdims=True))
        l_w = jnp.sum(e_w, axis=1, keepdims=True)
        e_w = e_w.astype(BF16)
        o_w = _dot(e_w[:, 0:tq], tiles[0])
        for c in range(1, ntile):
            o_w = o_w + _dot(e_w[:, c * tq:(c + 1) * tq], tiles[c])
        o_win.append(o_w / l_w)

    def sel_step(j, carry):
        sl = pl.ds(pl.multiple_of(j * tk, tk), tk)
        scores = [_dot_nt(q_aug[g], ka_ref[0, g, sl, :]) for g in range(NSA_GROUPS)]
        return tuple(_flash_update(*carry[g], scores[g], va_ref[0, g, sl, :]) for g in range(NSA_GROUPS))

    init = tuple((jnp.full((rows, 1), NEG_INF, F32), jnp.zeros((rows, LANES), F32)) for _ in range(NSA_GROUPS))
    carry = lax.fori_loop(0, start // tk + 1, sel_step, init)
    own_keys = pl.ds(pl.multiple_of(start, tq), tq)
    scores = [jnp.where(tri, _dot_nt(qs[g], ka_ref[0, g, own_keys, 0:LANES]), NEG_INF) for g in range(NSA_GROUPS)]
    carry = tuple(_flash_update(*carry[g], scores[g], va_ref[0, g, own_keys, :]) for g in range(NSA_GROUPS))

    outs = []
    for g in range(NSA_GROUPS):
        o_s = _flash_finish(carry[g][1], NSA_DIM)
        for r in range(rep):
            hh = g * rep + r
            rs = slice(r * tq, (r + 1) * tq)
            outs.append(gates[:, 3 * hh:3 * hh + 1] * o_cmp[g][rs, NSA_DIM:]
                        + gates[:, 3 * hh + 1:3 * hh + 2] * o_s[rs]
                        + gates[:, 3 * hh + 2:3 * hh + 3] * o_win[g][rs, NSA_DIM:])
    o_ref[0] = jnp.concatenate(outs, axis=1).astype(o_ref.dtype)


def _nsa_prompt(q, ka, va, win, pool, gates, tq, tk):
    nb, nh, ns, _ = q.shape
    nblk = ns // CMP_BLOCK
    assert WINDOW % tq == 0 and tk % tq == 0
    kern = functools.partial(_nsa_prompt_kernel, tq=tq, tk=tk)
    whole = lambda a: pl.BlockSpec((1,) + a.shape[1:], lambda b, i: (b, 0, 0, 0))
    return pl.pallas_call(
        kern, grid=(nb, ns // tq),
        in_specs=[pl.BlockSpec((1, nh, tq, LANES), lambda b, i: (b, 0, i, 0)),
                  whole(ka), whole(va), whole(win),
                  pl.BlockSpec((1, NSA_GROUPS, nblk, LANES), lambda b, i: (b, 0, 0, 0)),
                  pl.BlockSpec((1, tq, LANES), lambda b, i: (b, i, 0))],
        out_specs=pl.BlockSpec((1, tq, nh * NSA_DIM), lambda b, i: (b, i, 0)),
        out_shape=jax.ShapeDtypeStruct((nb, ns, nh * NSA_DIM), BF16),
        compiler_params=_params(("arbitrary", "arbitrary")), name="nsa_prompt",
    )(q, ka, va, win, pool, gates)


def _head_matmul_kernel(a_ref, b_ref, o_ref):
    o_ref[0] = _dot(a_ref[0], b_ref[0]).astype(o_ref.dtype)


def _head_matmul(a, b, dtype):
    nh, m, k = a.shape
    n = b.shape[2]
    return pl.pallas_call(
        _head_matmul_kernel, grid=(nh,),
        in_specs=[pl.BlockSpec((1, m, k), lambda h: (h, 0, 0)), pl.BlockSpec((1, k, n), lambda h: (h, 0, 0))],
        out_specs=pl.BlockSpec((1, m, n), lambda h: (h, 0, 0)),
        out_shape=jax.ShapeDtypeStruct((nh, m, n), dtype),
        compiler_params=_params(("arbitrary",)), name="head_matmul",
    )(a, b)


def _new_token_mask(rows, t_new):
    tok = lax.broadcasted_iota(jnp.int32, (rows, LANES), 0) % t_new
    lane = lax.broadcasted_iota(jnp.int32, (rows, LANES), 1)
    return (lane < t_new) & (lane <= tok)


def _page_specs(pages, feat):
    return [pl.BlockSpec((1, feat, PAGE_SIZE), lambda b, j, pt, p=p: (pt[b, j * pages + p], 0, 0))
            for p in range(pages)]


def _mla_sample_kernel(pt_ref, q_ref, new_ref, *rest, pages, t_new):
    page_refs, (o_ref, kv_scr, new_scr, m_scr, l_scr, acc_scr) = rest[:pages], rest[pages:]
    j = pl.program_id(1)
    width = kv_scr.shape[0]

    @pl.when((pl.program_id(0) == 0) & (j == 0))
    def _():
        kv_scr[...] = jnp.zeros(kv_scr.shape, BF16)

    @pl.when(j == 0)
    def _():
        m0, l0, a0 = _flash_init(q_ref.shape[1], width)
        m_scr[...], l_scr[...], acc_scr[...] = m0, l0, a0

    for p in range(pages):
        kv_scr[0:MLA_ROW, p * PAGE_SIZE:(p + 1) * PAGE_SIZE] = page_refs[p][0].astype(BF16)
    q = q_ref[0]
    kv = kv_scr[...]
    carry = _online_update((m_scr[...], l_scr[...], acc_scr[...]), _dot(q, kv), lambda p: _dot_nt(p, kv))
    m_scr[...], l_scr[...], acc_scr[...] = carry

    @pl.when(j == pl.num_programs(1) - 1)
    def _():
        new_scr[...] = jnp.zeros(new_scr.shape, BF16)
        new_scr[0:t_new, 0:MLA_ROW] = new_ref[0].astype(BF16)
        kvn = new_scr[...]
        s = jnp.where(_new_token_mask(q.shape[0], t_new), _dot_nt(q, kvn), NEG_INF)
        _, l, acc = _online_update(carry, s, lambda p: _dot(p, kvn))
        o_ref[0] = acc / l


def _mla_sample(page_table, q, new_rows, cache_t, pages):
    nb, rows, width = q.shape
    t_new = new_rows.shape[1]
    n_pages = page_table.shape[1]
    kern = functools.partial(_mla_sample_kernel, pages=pages, t_new=t_new)
    grid_spec = pltpu.PrefetchScalarGridSpec(
        num_scalar_prefetch=1, grid=(nb, n_pages // pages),
        in_specs=[pl.BlockSpec((1, rows, width), lambda b, j, pt: (b, 0, 0)),
                  pl.BlockSpec((1, t_new, MLA_ROW), lambda b, j, pt: (b, 0, 0))] + _page_specs(pages, MLA_ROW),
        out_specs=pl.BlockSpec((1, rows, width), lambda b, j, pt: (b, 0, 0)),
        scratch_shapes=[pltpu.VMEM((width, pages * PAGE_SIZE), BF16), pltpu.VMEM((LANES, width), BF16),
                        pltpu.VMEM((rows, 1), F32), pltpu.VMEM((rows, 1), F32), pltpu.VMEM((rows, width), F32)])
    return pl.pallas_call(
        kern, grid_spec=grid_spec, out_shape=jax.ShapeDtypeStruct((nb, rows, width), F32),
        compiler_params=_params(("arbitrary", "arbitrary")), name="mla_sample",
    )(page_table, q, new_rows, *([cache_t] * pages))


def _cmp_pool_kernel(pt_ref, w_ref, e_ref, *rest, pages):
    page_refs, (o_ref, hi_scr, lo_scr) = rest[:pages], rest[pages:]
    w = w_ref[...]
    for p in range(pages):
        x = page_refs[p][0] * w
        hi = x.astype(BF16)
        sl = slice(p * PAGE_SIZE, (p + 1) * PAGE_SIZE)
        hi_scr[:, sl] = hi
        lo_scr[:, sl] = (x - hi.astype(F32)).astype(BF16)
    e = e_ref[...]
    o_ref[0] = _dot_nt(e, hi_scr[...]) + _dot_nt(e, lo_scr[...])


def _cmp_pool(page_table, w_pool_t, cache_t, pages):
    nb, n_pages = page_table.shape
    feat = cache_t.shape[1]
    per_step = pages * PAGE_SIZE // CMP_BLOCK
    expand = _block_expand(per_step, pages * PAGE_SIZE)
    kern = functools.partial(_cmp_pool_kernel, pages=pages)
    grid_spec = pltpu.PrefetchScalarGridSpec(
        num_scalar_prefetch=1, grid=(nb, n_pages // pages),
        in_specs=[pl.BlockSpec(w_pool_t.shape, lambda b, j, pt: (0, 0)),
                  pl.BlockSpec(expand.shape, lambda b, j, pt: (0, 0))] + _page_specs(pages, feat),
        out_specs=pl.BlockSpec((1, per_step, feat), lambda b, j, pt: (b, j, 0)),
        scratch_shapes=[pltpu.VMEM((feat, pages * PAGE_SIZE), BF16), pltpu.VMEM((feat, pages * PAGE_SIZE), BF16)])
    return pl.pallas_call(
        kern, grid_spec=grid_spec,
        out_shape=jax.ShapeDtypeStruct((nb, n_pages * PAGE_SIZE // CMP_BLOCK, feat), F32),
        compiler_params=_params(("arbitrary", "arbitrary")), name="cmp_pool",
    )(page_table, w_pool_t, expand, *([cache_t] * pages))


def _pack_group(rows, g):
    ko, vo = g * NSA_DIM, NSA_GROUPS * NSA_DIM + g * NSA_DIM
    return jnp.concatenate([rows[:, ko:ko + NSA_DIM], rows[:, vo:vo + NSA_DIM]], axis=1).astype(BF16)


def _pack_group_t(feats, g):
    ko, vo = g * NSA_DIM, NSA_GROUPS * NSA_DIM + g * NSA_DIM
    return jnp.concatenate([feats[ko:ko + NSA_DIM], feats[vo:vo + NSA_DIM]], axis=0).astype(BF16)


def _nsa_sample_kernel(pt_ref, q_ref, pool_ref, gate_ref, win_ref, wnew_ref, snew_ref, expand_ref, *rest,
                       pages, t_new):
    page_refs = rest[:pages]
    o_ref, kv_scr, new_scr, q_scr, sel_scr, oc_scr, ow_scr, m_scr, l_scr, acc_scr = rest[pages:]
    j = pl.program_id(1)
    rep = NSA_REP
    rows = rep * t_new
    nblk = pool_ref.shape[1]
    tk = pages * PAGE_SIZE
    new_ok = _new_token_mask(rows, t_new)

    def new_tile(ref, g):
        new_scr[...] = jnp.zeros(new_scr.shape, BF16)
        new_scr[0:t_new, :] = _pack_group(ref[0], g)
        return new_scr[...]

    @pl.when(j == 0)
    def _():
        pool = pool_ref[0]
        win_t = win_ref[0]
        wb = win_t.shape[1]
        tok = lax.broadcasted_iota(jnp.int32, (rows, 1), 0) % t_new
        lane = lax.broadcasted_iota(jnp.int32, (nblk, LANES), 1)
        blk = lax.broadcasted_iota(jnp.int32, (nblk, LANES), 0)
        for g in range(NSA_GROUPS):
            q = q_ref[0, g * rows:(g + 1) * rows]
            pg = _pack_group(pool, g)
            oc_scr[g] = _dot(_softmax(_dot_nt(q, pg), axis=1).astype(BF16), pg)
            q_scr[...] = jnp.zeros(q_scr.shape, BF16)
            q_scr[0:rows, :] = q
            p_t = _softmax(_dot_nt(pg, q_scr[...]), axis=0)
            tot = p_t
            for r in range(1, rep):
                tot = tot + pltpu.roll(p_t, LANES - r * t_new, 1)
            tot = jnp.where(lane < t_new, tot, 0.0)
            imp = tot
            for r in range(1, rep):
                imp = imp + pltpu.roll(tot, r * t_new, 1)
            forced = (blk == 0) | (blk == nblk - 1)
            sel = forced | _top_blocks(jnp.where(forced, -jnp.inf, imp), SEL_TOPK - 3)
            sel_scr[g] = jnp.where(sel, 1.0, 0.0).T[0:rows]
            wg = _pack_group_t(win_t, g)
            i = lax.broadcasted_iota(jnp.int32, (1, wb), 1)
            ng = new_tile(wnew_ref, g)
            s_w = jnp.concatenate([jnp.where(i > tok + (wb - WINDOW), _dot(q, wg), NEG_INF),
                                   jnp.where(new_ok, _dot_nt(q, ng), NEG_INF)], axis=1)
            p_w = _softmax(s_w, axis=1).astype(BF16)
            ow_scr[g] = _dot_nt(p_w[:, :wb], wg) + _dot(p_w[:, wb:], ng)
            m0, l0, a0 = _flash_init(rows, LANES)
            m_scr[g], l_scr[g], acc_scr[g] = m0, l0, a0

    for p in range(pages):
        page = page_refs[p][0]
        for g in range(NSA_GROUPS):
            kv_scr[g, :, p * PAGE_SIZE:(p + 1) * PAGE_SIZE] = _pack_group_t(page, g)
    sl = pl.ds(pl.multiple_of(j * tk, tk), tk)
    for g in range(NSA_GROUPS):
        q = q_ref[0, g * rows:(g + 1) * rows]
        kv = kv_scr[g]
        hit = _dot(sel_scr[g].astype(BF16), expand_ref[:, sl])
        s = _dot(q, kv) + (hit - 1.0) * (-NEG_INF)
        carry = _online_update((m_scr[g], l_scr[g], acc_scr[g]), s, lambda p, kv=kv: _dot_nt(p, kv))
        m_scr[g], l_scr[g], acc_scr[g] = carry

    @pl.when(j == pl.num_programs(1) - 1)
    def _():
        gates = gate_ref[0]
        for g in range(NSA_GROUPS):
            q = q_ref[0, g * rows:(g + 1) * rows]
            ng = new_tile(snew_ref, g)
            s = jnp.where(new_ok, _dot_nt(q, ng), NEG_INF)
            _, l, acc = _online_update((m_scr[g], l_scr[g], acc_scr[g]), s, lambda p, ng=ng: _dot(p, ng))
            gg = gates[g * rows:(g + 1) * rows]
            o_ref[0, g * rows:(g + 1) * rows, :] = (gg[:, 0:1] * oc_scr[g] + gg[:, 1:2] * (acc / l)
                                                    + gg[:, 2:3] * ow_scr[g])


def _nsa_sample(page_table, q, pool, gates, win_t, win_new, slc_new, expand, cache_t, pages):
    nb, rows2, _ = q.shape
    t_new = slc_new.shape[1]
    n_pages = page_table.shape[1]
    feat = cache_t.shape[1]
    rows = rows2 // NSA_GROUPS
    per_b = lambda shape: pl.BlockSpec((1,) + shape, lambda b, j, pt: (b, 0, 0))
    kern = functools.partial(_nsa_sample_kernel, pages=pages, t_new=t_new)
    grp = lambda *shape: pltpu.VMEM((NSA_GROUPS,) + shape, F32)
    grid_spec = pltpu.PrefetchScalarGridSpec(
        num_scalar_prefetch=1, grid=(nb, n_pages // pages),
        in_specs=[per_b((rows2, LANES)), per_b(pool.shape[1:]), per_b((rows2, LANES)), per_b(win_t.shape[1:]),
                  per_b((t_new, feat)), per_b((t_new, feat)),
                  pl.BlockSpec(expand.shape, lambda b, j, pt: (0, 0))] + _page_specs(pages, feat),
        out_specs=per_b((rows2, LANES)),
        scratch_shapes=[pltpu.VMEM((NSA_GROUPS, LANES, pages * PAGE_SIZE), BF16), pltpu.VMEM((LANES, LANES), BF16),
                        pltpu.VMEM((LANES, LANES), BF16), grp(rows, expand.shape[0]),
                        grp(rows, LANES), grp(rows, LANES), grp(rows, 1), grp(rows, 1), grp(rows, LANES)])
    return pl.pallas_call(
        kern, grid_spec=grid_spec, out_shape=jax.ShapeDtypeStruct((nb, rows2, LANES), F32),
        compiler_params=_params(("arbitrary", "arbitrary")), name="nsa_sample",
    )(page_table, q, pool, gates, win_t, win_new, slc_new, expand, *([cache_t] * pages))


def _tail_kernel(x_ref, ya_ref, yb_ref, ga_ref, gb_ref, p_ref, wba_ref, wbn_ref, wo_ref, lnf_ref,
                 wg_ref, wu_ref, wd_ref, lnp_ref, wpg_ref, wpp_ref, lnl_ref, o_ref, hid_scr, *, chunk):
    merged = ga_ref[...] * _dot(ya_ref[...], wba_ref[...]) + gb_ref[...] * _dot(yb_ref[...], wbn_ref[...])
    x = x_ref[...] + _dot(merged.astype(BF16), wo_ref[...])
    h = _rms(x, lnf_ref[...]).astype(BF16)

    def ffn(c, carry):
        sl = pl.ds(pl.multiple_of(c * chunk, chunk), chunk)
        hid_scr[:, sl] = (jax.nn.silu(_dot(h, wg_ref[:, sl])) * _dot(h, wu_ref[:, sl])).astype(BF16)
        return carry

    lax.fori_loop(0, wg_ref.shape[1] // chunk, ffn, 0)
    x = x + _dot(hid_scr[...], wd_ref[...])
    gate = jax.nn.sigmoid(_dot(_rms(x, lnp_ref[...]).astype(BF16), wpg_ref[...]))
    x = x + gate * _dot(p_ref[...].astype(BF16), wpp_ref[...])
    o_ref[...] = _rms(x, lnl_ref[...])


def _tail(x, ya, yb, ga, gb, p, weights, tm):
    n, d_model = x.shape
    row = lambda a: pl.BlockSpec((tm, a.shape[1]), lambda i: (i, 0))
    acts = (x, ya, yb, ga, gb, p)
    kern = functools.partial(_tail_kernel, chunk=2 * LANES)
    return pl.pallas_call(
        kern, grid=(n // tm,),
        in_specs=[row(a) for a in acts] + [_resident(w.shape) for w in weights],
        out_specs=pl.BlockSpec((tm, d_model), lambda i: (i, 0)),
        out_shape=jax.ShapeDtypeStruct((n, d_model), F32),
        scratch_shapes=[pltpu.VMEM((tm, weights[4].shape[1]), BF16)],
        compiler_params=_params(("arbitrary",)), name="tail",
    )(*acts, *weights)


def _rope_tables(pos):
    pos = pos.astype(F32)[:, None]
    n = pos.shape[0]

    def cs(rot):
        half = rot // 2
        inv = jnp.float32(ROPE_THETA) ** (-jnp.arange(half, dtype=F32) * 2.0 / rot)
        ang = pos * inv[None, :]
        return jnp.cos(ang), jnp.sin(ang)

    one = lambda w: jnp.ones((n, w), F32)
    zero = lambda w: jnp.zeros((n, w), F32)
    c, s = cs(MLA_ROPE)
    pad = LANES - MLA_NOPE - MLA_ROPE
    cm = jnp.concatenate([one(MLA_NOPE), c, c, one(pad)], axis=1)
    sm = jnp.concatenate([zero(MLA_NOPE), -s, s, zero(pad)], axis=1)
    c, s = cs(NSA_ROT)
    rest = NSA_DIM - NSA_ROT
    cn = jnp.concatenate([c, c, one(rest)] * (LANES // NSA_DIM), axis=1)
    sn = jnp.concatenate([-s, s, zero(rest)] * (LANES // NSA_DIM), axis=1)
    return cm, sm, cn, sn


def _layout_w_in(w_in):
    d_model = w_in.shape[0]
    sizes = (MLA_HEADS * (MLA_NOPE + MLA_ROPE), MLA_KV_LORA, MLA_ROPE, NSA_HEADS * NSA_DIM,
             SEG_KV // 3, SEG_KV // 3, SEG_KV // 3, 3 * NSA_HEADS, 2 * d_model)
    offs = np.concatenate([[0], np.cumsum(sizes)])
    qa, ca, ra, qb, zc, zs, zw, gn, gm = [w_in[:, int(offs[i]):int(offs[i + 1])] for i in range(len(sizes))]
    qa = qa.reshape(d_model, MLA_HEADS, MLA_NOPE + MLA_ROPE)
    qa = jnp.pad(qa, ((0, 0), (0, 0), (0, LANES - MLA_NOPE - MLA_ROPE))).reshape(d_model, SEG_QA)
    ra = jnp.pad(ra, ((0, 0), (MLA_NOPE, LANES - MLA_NOPE - MLA_ROPE)))
    gn = jnp.pad(gn, ((0, 0), (0, SEG_GN - gn.shape[1])))
    return jnp.concatenate([qa, ca, ra, qb, zc, zs, zw, gn, gm], axis=1).astype(BF16)


def _layout_w_up(w_uk, w_uv):
    wk = jnp.pad(w_uk, ((0, 0), (0, 0), (0, LANES - MLA_NOPE))).reshape(MLA_KV_LORA, SEG_QA)
    return jnp.concatenate([wk, w_uv.reshape(MLA_KV_LORA, MLA_HEADS * MLA_V)], axis=1).astype(BF16)


def _layout_absorb(w_uk, w_uv, width):
    to_lat = jnp.zeros((MLA_HEADS, LANES, width), F32)
    to_lat = to_lat.at[:, 0:MLA_NOPE, 0:MLA_KV_LORA].set(w_uk.transpose(1, 2, 0))
    to_lat = to_lat.at[:, MLA_NOPE:MLA_NOPE + MLA_ROPE, MLA_KV_LORA:MLA_ROW].set(jnp.eye(MLA_ROPE, dtype=F32))
    to_val = jnp.zeros((MLA_HEADS, width, MLA_V), F32).at[:, 0:MLA_KV_LORA, :].set(w_uv.transpose(1, 0, 2))
    return to_lat.astype(BF16), to_val.astype(BF16)


def _block_expand(nblk, nkeys):
    return (jnp.arange(nkeys)[None, :] // CMP_BLOCK == jnp.arange(nblk)[:, None]).astype(BF16)


def kernel(x_prompt, x_sample, cache_mla, cache_nsa_cmp, cache_nsa_slc, state_nsa_win, page_table, p_prompt, p_sample, ln_attn, w_in, mla_kv_norm, mla_w_uk, mla_w_uv, nsa_w_cmp_k, nsa_w_cmp_v, w_branch_mla, w_branch_nsa, w_out, ln_ffn, w_ffn_gate, w_ffn_up, w_ffn_down, ln_ple, w_ple_gate, w_ple_proj, ln_final):
    nb, ns, d_model = x_prompt.shape
    db, t_new, _ = x_sample.shape
    depth = w_in.shape[0]
    n_pages = page_table.shape[1]
    past = n_pages * PAGE_SIZE
    wb = state_nsa_win.shape[2]
    width = 2 * NSA_GROUPS * NSA_DIM
    g_, d_ = NSA_GROUPS, NSA_DIM
    assert depth == 1 and t_new <= CMP_BLOCK and past % CMP_BLOCK == 0 and wb == WINDOW and past >= wb
    assert ns % 512 == 0 and (db * t_new) % 512 == 0 and NSA_REP * t_new <= LANES
    tm = 512
    pages = min(64, n_pages)
    lat_w = 3 * LANES
    i = 0

    w_all = _layout_w_in(w_in[i])
    w_up = _layout_w_up(mla_w_uk[i], mla_w_uv[i])
    to_lat, to_val = _layout_absorb(mla_w_uk[i], mla_w_uv[i], lat_w)
    w_pool = jnp.concatenate([nsa_w_cmp_k[i]] * g_ + [nsa_w_cmp_v[i]] * g_, axis=1)
    w_pool_t = jnp.tile(w_pool.T, (1, PAGE_SIZE // CMP_BLOCK))
    ln_a = ln_attn[i][None, :]
    kvn = mla_kv_norm[i][None, :]
    tail_w = (w_branch_mla[i].astype(BF16), w_branch_nsa[i].astype(BF16), w_out[i].astype(BF16),
              ln_ffn[i][None, :], w_ffn_gate[i].astype(BF16), w_ffn_up[i].astype(BF16),
              w_ffn_down[i].astype(BF16), ln_ple[i][None, :], w_ple_gate[i].astype(BF16),
              w_ple_proj[i].astype(BF16), ln_final[None, :])

    xp = x_prompt.reshape(nb * ns, d_model)
    nblk = ns // CMP_BLOCK
    (qm, km, vm, mla_p, qn, kvc_p, kvs_p, kvw_p, ka, va, win, pool, gn, ga, gb) = _proj(
        xp, _rope_tables(jnp.arange(ns)), ln_a, w_all, w_up, kvn, w_pool, nb, ns, tm, nblk)
    token_major = lambda a: a.transpose(0, 2, 1).reshape(-1, a.shape[1])
    mla_p, kvc_p, kvs_p, kvw_p = (token_major(a) for a in (mla_p, kvc_p, kvs_p, kvw_p))
    ya = _mla_flash(qm, km, vm, 512)
    pool5 = pool.reshape(nb, nblk, 2, g_, d_)
    pool_g = jnp.concatenate([pool5[:, :, 0].transpose(0, 2, 1, 3), pool5[:, :, 1].transpose(0, 2, 1, 3)], axis=-1)
    yb = _nsa_prompt(qn, ka, va, win, pool_g, gn.reshape(nb, ns, LANES), 128, 512)
    y_prompt = _tail(xp, ya.reshape(nb * ns, -1), yb.reshape(nb * ns, -1), ga, gb,
                     p_prompt[i].reshape(nb * ns, -1), tail_w, tm).reshape(nb, ns, d_model)

    n_s = db * t_new
    xs = x_sample.reshape(n_s, d_model)
    pos_s = jnp.tile(past + jnp.arange(t_new), db)
    (qm, _, _, mla_s, qn, kvc_s, kvs_s, kvw_s, _, _, _, _, gn, ga, gb) = _proj(
        xs, _rope_tables(pos_s), ln_a, w_all, w_up, kvn, w_pool, 1, n_s, tm, LANES)
    mla_s, kvc_s, kvs_s, kvw_s = (token_major(a) for a in (mla_s, kvc_s, kvs_s, kvw_s))
    by_batch = lambda a: a.reshape(a.shape[0], db, t_new, a.shape[-1]).transpose(1, 0, 2, 3).reshape(
        db, a.shape[0] * t_new, a.shape[-1])
    q_lat = by_batch(_head_matmul(qm[0], to_lat, BF16))
    mla_t = cache_mla[i].transpose(0, 2, 1)
    o_lat = _mla_sample(page_table, q_lat, mla_s.reshape(db, t_new, MLA_ROW), mla_t, pages)
    o_lat = o_lat.reshape(db, MLA_HEADS, t_new, lat_w).transpose(1, 0, 2, 3).reshape(MLA_HEADS, n_s, lat_w)
    ya = _head_matmul(o_lat.astype(BF16), to_val, BF16).transpose(1, 0, 2).reshape(n_s, MLA_HEADS * MLA_V)

    cmp_t = cache_nsa_cmp[i].reshape(-1, PAGE_SIZE, width).transpose(0, 2, 1)
    slc_t = cache_nsa_slc[i].reshape(-1, PAGE_SIZE, width).transpose(0, 2, 1)
    pool_s = _cmp_pool(page_table, w_pool_t, cmp_t, pages)
    gate_s = gn[:, :3 * NSA_HEADS].reshape(db, t_new, NSA_HEADS, 3).transpose(0, 2, 1, 3).reshape(
        db, NSA_HEADS * t_new, 3)
    gate_s = jnp.pad(gate_s, ((0, 0), (0, 0), (0, LANES - 3)))
    win_state = state_nsa_win[i].reshape(db, wb, width)
    o_nsa = _nsa_sample(page_table, by_batch(qn[0]), pool_s, gate_s, win_state.transpose(0, 2, 1),
                        kvw_s.reshape(db, t_new, width), kvs_s.reshape(db, t_new, width),
                        _block_expand(past // CMP_BLOCK, past), slc_t, pages)
    yb = o_nsa[:, :, NSA_DIM:].reshape(db, NSA_HEADS, t_new, d_).transpose(0, 2, 1, 3).reshape(n_s, -1)
    y_sample = _tail(xs, ya, yb.astype(BF16), ga, gb, p_sample[i].reshape(n_s, -1), tail_w, tm).reshape(
        db, t_new, d_model)

    kv5 = lambda a, b, s: a.reshape(1, b, s, 2, g_, d_)
    win_all = jnp.concatenate([win_state, kvw_s.reshape(db, t_new, width)], axis=1)
    wkeep = min(WINDOW, ns)
    return (y_prompt, y_sample,
            mla_p.reshape(1, nb, ns, MLA_ROW), mla_s.reshape(1, db, t_new, MLA_ROW),
            kv5(kvc_p, nb, ns), kv5(kvc_s, db, t_new), kv5(kvs_p, nb, ns), kv5(kvs_s, db, t_new),
            kv5(kvw_p, nb, ns)[:, :, ns - wkeep:],
            kv5(win_all[:, -min(WINDOW, past + t_new):], db, min(WINDOW, past + t_new)))
```

```python
import functools
import math

import jax
import jax.numpy as jnp
import numpy as np
from jax import lax
from jax.experimental import pallas as pl
from jax.experimental.pallas import tpu as pltpu

F32 = jnp.float32
BF16 = jnp.bfloat16

NORM_EPS = 1e-6
ROPE_THETA = 500000.0
NEG_INF = -1e30
LOG2E = math.log2(math.e)
PAGE_SIZE = 128
MLA_HEADS = 8
MLA_NOPE = 64
MLA_ROPE = 32
MLA_V = 64
MLA_KV_LORA = 256
MLA_ROW = MLA_KV_LORA + MLA_ROPE
NSA_HEADS = 8
NSA_GROUPS = 2
NSA_REP = NSA_HEADS // NSA_GROUPS
NSA_DIM = 64
NSA_ROT = NSA_DIM // 4
CMP_BLOCK = 64
SEL_TOPK = 16
WINDOW = 512
LANES = 128
VMEM_LIMIT = 60 * 1024 * 1024

SEG_QA = MLA_HEADS * LANES
SEG_C = MLA_KV_LORA
SEG_R = LANES
SEG_QB = NSA_HEADS * NSA_DIM
SEG_KV = 3 * 2 * NSA_GROUPS * NSA_DIM
SEG_GN = LANES
OFF_C = SEG_QA
OFF_R = OFF_C + SEG_C
OFF_QB = OFF_R + SEG_R
OFF_KV = OFF_QB + SEG_QB
OFF_GN = OFF_KV + SEG_KV
OFF_GM = OFF_GN + SEG_GN


def _params(sem):
    return pltpu.CompilerParams(dimension_semantics=sem, vmem_limit_bytes=VMEM_LIMIT)


def _resident(shape):
    nd = len(shape)
    return pl.BlockSpec(shape, lambda *_: (0,) * nd, pipeline_mode=pl.Buffered(1))


def _dot(a, b):
    return jnp.dot(a, b, preferred_element_type=F32)


def _dot_nt(a, b):
    return lax.dot_general(a, b, (((1,), (1,)), ((), ())), preferred_element_type=F32)


def _rms(x, g):
    return x * lax.rsqrt(jnp.mean(x * x, axis=-1, keepdims=True) + NORM_EPS) * g


def _value_slot(v, rows):
    lane = lax.broadcasted_iota(jnp.int32, (rows, LANES - v.shape[1]), 1)
    return jnp.concatenate([v, jnp.where(lane == 0, 1.0, 0.0)], axis=1).astype(BF16)


def _proj_kernel(x_ref, ln_ref, w_ref, wup_ref, kvn_ref, cm_ref, sm_ref, cn_ref, sn_ref, wpool_ref,
                 qm_ref, km_ref, vm_ref, mla_ref, qn_ref, kvc_ref, kvs_ref, kvw_ref,
                 ka_ref, va_ref, win_ref, pool_ref, gn_ref, ga_ref, gb_ref, *, nt):
    tm = x_ref.shape[0]
    d_model = x_ref.shape[1]
    nblk = ka_ref.shape[3] - LANES
    h = _rms(x_ref[...], ln_ref[...]).astype(BF16)
    lane = lax.broadcasted_iota(jnp.int32, (tm, LANES), 1)
    cm, sm, cn, sn = cm_ref[...], sm_ref[...], cn_ref[...], sn_ref[...]

    def rope_m(z):
        half = MLA_ROPE // 2
        sw = jnp.where(lane < MLA_NOPE + half, pltpu.roll(z, LANES - half, 1), pltpu.roll(z, half, 1))
        return z * cm + sw * sm

    def rope_n(z):
        half = NSA_ROT // 2
        sw = jnp.where((lane & (NSA_DIM - 1)) < half, pltpu.roll(z, LANES - half, 1), pltpu.roll(z, half, 1))
        return z * cn + sw * sn

    zq = _dot(h, w_ref[:, 0:SEG_QA])
    scale_m = (MLA_NOPE + MLA_ROPE) ** -0.5 * LOG2E
    for hh in range(MLA_HEADS):
        qm_ref[0, hh] = (rope_m(zq[:, hh * LANES:(hh + 1) * LANES]) * scale_m).astype(BF16)

    zc = _dot(h, w_ref[:, OFF_C:OFF_R])
    c = _rms(zc, kvn_ref[...])
    kpe = rope_m(_dot(h, w_ref[:, OFF_R:OFF_QB]))
    mla_ref[0, 0:MLA_KV_LORA, :] = c.T
    mla_ref[0, MLA_KV_LORA:MLA_ROW, :] = kpe.T[MLA_NOPE:MLA_NOPE + MLA_ROPE]
    up = _dot(c.astype(BF16), wup_ref[...])
    for hh in range(MLA_HEADS):
        km_ref[0, hh] = (up[:, hh * LANES:(hh + 1) * LANES] + kpe).astype(BF16)
        vo = SEG_QA + hh * MLA_V
        vm_ref[0, hh] = _value_slot(up[:, vo:vo + MLA_V], tm)

    zqb = _dot(h, w_ref[:, OFF_QB:OFF_KV])
    zero_half = jnp.zeros((tm, NSA_DIM), F32)
    scale_n = NSA_DIM ** -0.5 * LOG2E
    for pair in range(NSA_HEADS // 2):
        zz = rope_n(zqb[:, pair * LANES:(pair + 1) * LANES]) * scale_n
        qn_ref[0, 2 * pair] = jnp.concatenate([zz[:, :NSA_DIM], zero_half], axis=1).astype(BF16)
        qn_ref[0, 2 * pair + 1] = jnp.concatenate([zz[:, NSA_DIM:], zero_half], axis=1).astype(BF16)

    zkv = _dot(h, w_ref[:, OFF_KV:OFF_GN])
    width = 2 * NSA_GROUPS * NSA_DIM

    def kv_rows(i):
        z = zkv[:, i * width:(i + 1) * width]
        return jnp.concatenate([rope_n(z[:, :LANES]), z[:, LANES:]], axis=1)

    rows_c, rows_s, rows_w = kv_rows(0), kv_rows(1), kv_rows(2)
    kvc_ref[0] = rows_c.T
    kvs_ref[0] = rows_s.T
    kvw_ref[0] = rows_w.T
    pos = (pl.program_id(0) % nt) * tm + lax.broadcasted_iota(jnp.int32, (tm, nblk), 0)
    onehot = jnp.where(lax.broadcasted_iota(jnp.int32, (tm, nblk), 1) == pos // CMP_BLOCK, 1.0, 0.0)
    for g in range(NSA_GROUPS):
        ko, vo = g * NSA_DIM, LANES + g * NSA_DIM
        ka_ref[0, g] = jnp.concatenate([rows_s[:, ko:ko + NSA_DIM], zero_half, onehot], axis=1).astype(BF16)
        va_ref[0, g] = _value_slot(rows_s[:, vo:vo + NSA_DIM], tm)
        win_ref[0, g] = jnp.concatenate([rows_w[:, ko:ko + NSA_DIM], rows_w[:, vo:vo + NSA_DIM]],
                                        axis=1).astype(BF16)
    pool_ref[...] = jnp.sum(rows_c.reshape(tm // CMP_BLOCK, CMP_BLOCK, width) * wpool_ref[...][None], axis=1)

    gn_ref[...] = jax.nn.sigmoid(_dot(h, w_ref[:, OFF_GN:OFF_GM]))
    zm = _dot(h, w_ref[:, OFF_GM:OFF_GM + 2 * d_model])
    ga_ref[...] = jax.nn.sigmoid(zm[:, :d_model])
    gb_ref[...] = jax.nn.sigmoid(zm[:, d_model:])


def _proj(x, tabs, ln, w_all, w_up, kv_norm, w_pool, nb, ns, tm, nblk):
    n, d_model = x.shape
    nt = ns // tm
    width = 2 * NSA_GROUPS * NSA_DIM
    row = lambda w: pl.BlockSpec((tm, w), lambda i: (i, 0))
    head = lambda hn, w: pl.BlockSpec((1, hn, tm, w), lambda i: (i // nt, 0, i % nt, 0))
    tab = pl.BlockSpec((tm, LANES), lambda i: (i % nt, 0))
    feat = lambda f: pl.BlockSpec((1, f, tm), lambda i: (i // nt, 0, i % nt))
    slots = (
        (head(MLA_HEADS, LANES), (nb, MLA_HEADS, ns, LANES), BF16),
        (head(MLA_HEADS, LANES), (nb, MLA_HEADS, ns, LANES), BF16),
        (head(MLA_HEADS, LANES), (nb, MLA_HEADS, ns, LANES), BF16),
        (feat(MLA_ROW), (nb, MLA_ROW, ns), F32),
        (head(NSA_HEADS, LANES), (nb, NSA_HEADS, ns, LANES), BF16),
        (feat(width), (nb, width, ns), F32),
        (feat(width), (nb, width, ns), F32),
        (feat(width), (nb, width, ns), F32),
        (head(NSA_GROUPS, LANES + nblk), (nb, NSA_GROUPS, ns, LANES + nblk), BF16),
        (head(NSA_GROUPS, LANES), (nb, NSA_GROUPS, ns, LANES), BF16),
        (head(NSA_GROUPS, LANES), (nb, NSA_GROUPS, ns, LANES), BF16),
        (pl.BlockSpec((tm // CMP_BLOCK, width), lambda i: (i, 0)), (n // CMP_BLOCK, width), F32),
        (row(LANES), (n, LANES), F32),
        (row(d_model), (n, d_model), F32),
        (row(d_model), (n, d_model), F32),
    )
    in_specs = [row(d_model), _resident(ln.shape), _resident(w_all.shape), _resident(w_up.shape),
                _resident(kv_norm.shape), tab, tab, tab, tab, _resident(w_pool.shape)]
    return pl.pallas_call(
        functools.partial(_proj_kernel, nt=nt), grid=(n // tm,), in_specs=in_specs,
        out_specs=tuple(s[0] for s in slots),
        out_shape=tuple(jax.ShapeDtypeStruct(s[1], s[2]) for s in slots),
        compiler_params=_params(("arbitrary",)), name="proj",
    )(x, ln, w_all, w_up, kv_norm, *tabs, w_pool)


def _flash_update(m, acc, s, v):
    m_new = jnp.maximum(m, jnp.max(s, axis=-1, keepdims=True))
    alpha = jnp.exp2(m - m_new)
    p = jnp.exp2(s - m_new).astype(BF16)
    return m_new, alpha * acc + _dot(p, v)


def _flash_finish(acc, dv):
    return acc[:, :dv] / acc[:, dv:dv + 1]


def _online_update(carry, s, pv):
    m, l, acc = carry
    m_new = jnp.maximum(m, jnp.max(s, axis=-1, keepdims=True))
    alpha = jnp.exp2(m - m_new)
    p = jnp.exp2(s - m_new)
    l = alpha * l + jnp.sum(p, axis=-1, keepdims=True)
    acc = alpha * acc + pv(p.astype(BF16))
    return m_new, l, acc


def _flash_init(rows, dv):
    return (jnp.full((rows, 1), NEG_INF, F32), jnp.zeros((rows, 1), F32), jnp.zeros((rows, dv), F32))


def _mla_flash_kernel(q_ref, k_ref, v_ref, o_ref, *, t):
    qi = pl.program_id(2)
    heads = q_ref.shape[1]
    qs = [q_ref[0, hh] for hh in range(heads)]
    causal = lax.broadcasted_iota(jnp.int32, (t, t), 1) <= lax.broadcasted_iota(jnp.int32, (t, t), 0)

    def step(j, carry, mask):
        sl = pl.ds(pl.multiple_of(j * t, t), t)
        scores = [_dot_nt(qs[hh], k_ref[0, hh, sl, :]) for hh in range(heads)]
        if mask is not None:
            scores = [jnp.where(mask, s, NEG_INF) for s in scores]
        return tuple(_flash_update(*carry[hh], scores[hh], v_ref[0, hh, sl, :]) for hh in range(heads))

    init = tuple((jnp.full((t, 1), NEG_INF, F32), jnp.zeros((t, LANES), F32)) for _ in range(heads))
    carry = lax.fori_loop(0, qi, lambda j, c: step(j, c, None), init)
    carry = step(qi, carry, causal)
    o_ref[0] = jnp.concatenate([_flash_finish(acc, MLA_V) for _, acc in carry], axis=1).astype(o_ref.dtype)


def _mla_flash(q, k, v, t):
    nb, nh, ns, _ = q.shape
    hp = 4
    kern = functools.partial(_mla_flash_kernel, t=t)
    whole = pl.BlockSpec((1, hp, ns, LANES), lambda b, h, i: (b, h, 0, 0))
    return pl.pallas_call(
        kern, grid=(nb, nh // hp, ns // t),
        in_specs=[pl.BlockSpec((1, hp, t, LANES), lambda b, h, i: (b, h, i, 0)), whole, whole],
        out_specs=pl.BlockSpec((1, t, hp * MLA_V), lambda b, h, i: (b, i, h)),
        out_shape=jax.ShapeDtypeStruct((nb, ns, nh * MLA_V), BF16),
        compiler_params=_params(("arbitrary", "arbitrary", "arbitrary")), name="mla_flash",
    )(q, k, v)


def _masked_softmax(s, mask, axis):
    s = jnp.where(mask, s, NEG_INF)
    m = jnp.max(s, axis=axis, keepdims=True)
    e = jnp.where(mask, jnp.exp2(s - m), 0.0)
    return e / jnp.maximum(jnp.sum(e, axis=axis, keepdims=True), 1e-30)


def _softmax(s, axis):
    e = jnp.exp2(s - jnp.max(s, axis=axis, keepdims=True))
    return e / jnp.sum(e, axis=axis, keepdims=True)


def _top_blocks(imp, rounds):
    blk = lax.broadcasted_iota(jnp.int32, imp.shape, 0)
    sel = jnp.zeros(imp.shape, jnp.bool_)
    v = imp
    for _ in range(rounds):
        m = jnp.max(v, axis=0, keepdims=True)
        first = jnp.min(jnp.where(v == m, blk, imp.shape[0]), axis=0, keepdims=True)
        hit = blk == first
        sel = sel | hit
        v = jnp.where(hit, -jnp.inf, v)
    return sel


def _tile_heads(x, rep):
    return jnp.concatenate([x] * rep, axis=0)


def _nsa_prompt_kernel(q_ref, ka_ref, va_ref, win_ref, pool_ref, gate_ref, o_ref, *, tq, tk):
    qi = pl.program_id(1)
    nblk = pool_ref.shape[2]
    rep = NSA_REP
    rows = rep * tq
    start = qi * tq
    qpos4 = start + (lax.broadcasted_iota(jnp.int32, (rows, 1), 0) & (tq - 1))
    blk4 = lax.broadcasted_iota(jnp.int32, (rows, nblk), 1)
    blk_t = lax.broadcasted_iota(jnp.int32, (nblk, tq), 0)
    cur_t = (start + lax.broadcasted_iota(jnp.int32, (nblk, tq), 1)) // CMP_BLOCK
    blk_t4 = lax.broadcasted_iota(jnp.int32, (nblk, rows), 0)
    qpos_t4 = start + (lax.broadcasted_iota(jnp.int32, (nblk, rows), 1) & (tq - 1))
    forced = (blk_t == 0) | (blk_t == cur_t) | (blk_t == cur_t - 1)
    seen = blk_t <= cur_t
    own = (blk_t >= start // CMP_BLOCK) & (blk_t < (start + tq) // CMP_BLOCK)
    tri = (lax.broadcasted_iota(jnp.int32, (rows, tq), 1)
           <= (lax.broadcasted_iota(jnp.int32, (rows, tq), 0) & (tq - 1)))
    gates = gate_ref[0]

    ntile = WINDOW // tq + 1
    qs, pools, s_cmp, s_blk, s_win, win_tiles = [], [], [], [], [], []
    for g in range(NSA_GROUPS):
        q = q_ref[0, g * rep:(g + 1) * rep].reshape(rows, LANES)
        pool = pool_ref[0, g].astype(BF16)
        qs.append(q)
        pools.append(pool)
        s_cmp.append(_dot_nt(q, pool))
        s_blk.append(_dot_nt(pool, q))
        scores, tiles = [], []
        for c in range(ntile):
            idx = qi - (ntile - 1) + c
            sl = pl.ds(pl.multiple_of(jnp.maximum(idx, 0) * tq, tq), tq)
            kvw = win_ref[0, g, sl, :]
            s = _dot_nt(q, kvw)
            if c == ntile - 1:
                s = jnp.where(tri, s, NEG_INF)
            else:
                if c == 0:
                    s = jnp.where(tri, NEG_INF, s)
                s = s + jnp.where(idx >= 0, 0.0, NEG_INF)
            scores.append(s)
            tiles.append(kvw)
        s_win.append(jnp.concatenate(scores, axis=1))
        win_tiles.append(tiles)

    q_aug, o_cmp, o_win = [], [], []
    for g in range(NSA_GROUPS):
        q, pool, tiles = qs[g], pools[g], win_tiles[g]
        p_c = _masked_softmax(s_cmp[g], (blk4 + 1) * CMP_BLOCK - 1 <= qpos4, axis=1)
        o_cmp.append(_dot(p_c.astype(BF16), pool))
        p_t = _masked_softmax(s_blk[g], (blk_t4 + 1) * CMP_BLOCK - 1 <= qpos_t4, axis=0)
        imp = p_t[:, 0:tq]
        for r in range(1, rep):
            imp = imp + p_t[:, r * tq:(r + 1) * tq]
        imp = jnp.where(forced | ~seen, -jnp.inf, imp)
        sel = (forced | _top_blocks(imp, SEL_TOPK - 3)) & seen
        bias = jnp.where(sel & ~own, 0.0, NEG_INF).T
        q_aug.append(jnp.concatenate([q, _tile_heads(bias, rep).astype(BF16)], axis=1))
        s_w = s_win[g]
        e_w = jnp.exp2(s_w - jnp.max(s_w, axis=1, keepdims=True))
        l_w = jnp.sum(e_w, axis=1, keepdims=True)
        e_w = e_w.astype(BF16)
        o_w = _dot(e_w[:, 0:tq], tiles[0])
        for c in range(1, ntile):
            o_w = o_w + _dot(e_w[:, c * tq:(c + 1) * tq], tiles[c])
        o_win.append(o_w / l_w)

    def sel_step(j, carry):
        sl = pl.ds(pl.multiple_of(j * tk, tk), tk)
        scores = [_dot_nt(q_aug[g], ka_ref[0, g, sl, :]) for g in range(NSA_GROUPS)]
        return tuple(_flash_update(*carry[g], scores[g], va_ref[0, g, sl, :]) for g in range(NSA_GROUPS))

    init = tuple((jnp.full((rows, 1), NEG_INF, F32), jnp.zeros((rows, LANES), F32)) for _ in range(NSA_GROUPS))
    carry = lax.fori_loop(0, start // tk + 1, sel_step, init)
    own_keys = pl.ds(pl.multiple_of(start, tq), tq)
    scores = [jnp.where(tri, _dot_nt(qs[g], ka_ref[0, g, own_keys, 0:LANES]), NEG_INF) for g in range(NSA_GROUPS)]
    carry = tuple(_flash_update(*carry[g], scores[g], va_ref[0, g, own_keys, :]) for g in range(NSA_GROUPS))

    outs = []
    for g in range(NSA_GROUPS):
        o_s = _flash_finish(carry[g][1], NSA_DIM)
        for r in range(rep):
            hh = g * rep + r
            rs = slice(r * tq, (r + 1) * tq)
            outs.append(gates[:, 3 * hh:3 * hh + 1] * o_cmp[g][rs, NSA_DIM:]
                        + gates[:, 3 * hh + 1:3 * hh + 2] * o_s[rs]
                        + gates[:, 3 * hh + 2:3 * hh + 3] * o_win[g][rs, NSA_DIM:])
    o_ref[0] = jnp.concatenate(outs, axis=1).astype(o_ref.dtype)


def _nsa_prompt(q, ka, va, win, pool, gates, tq, tk):
    nb, nh, ns, _ = q.shape
    nblk = ns // CMP_BLOCK
    assert WINDOW % tq == 0 and tk % tq == 0
    kern = functools.partial(_nsa_prompt_kernel, tq=tq, tk=tk)
    whole = lambda a: pl.BlockSpec((1,) + a.shape[1:], lambda b, i: (b, 0, 0, 0))
    return pl.pallas_call(
        kern, grid=(nb, ns // tq),
        in_specs=[pl.BlockSpec((1, nh, tq, LANES), lambda b, i: (b, 0, i, 0)),
                  whole(ka), whole(va), whole(win),
                  pl.BlockSpec((1, NSA_GROUPS, nblk, LANES), lambda b, i: (b, 0, 0, 0)),
                  pl.BlockSpec((1, tq, LANES), lambda b, i: (b, i, 0))],
        out_specs=pl.BlockSpec((1, tq, nh * NSA_DIM), lambda b, i: (b, i, 0)),
        out_shape=jax.ShapeDtypeStruct((nb, ns, nh * NSA_DIM), BF16),
        compiler_params=_params(("arbitrary", "arbitrary")), name="nsa_prompt",
    )(q, ka, va, win, pool, gates)


def _head_matmul_kernel(a_ref, b_ref, o_ref):
    o_ref[0] = _dot(a_ref[0], b_ref[0]).astype(o_ref.dtype)


def _head_matmul(a, b, dtype):
    nh, m, k = a.shape
    n = b.shape[2]
    return pl.pallas_call(
        _head_matmul_kernel, grid=(nh,),
        in_specs=[pl.BlockSpec((1, m, k), lambda h: (h, 0, 0)), pl.BlockSpec((1, k, n), lambda h: (h, 0, 0))],
        out_specs=pl.BlockSpec((1, m, n), lambda h: (h, 0, 0)),
        out_shape=jax.ShapeDtypeStruct((nh, m, n), dtype),
        compiler_params=_params(("arbitrary",)), name="head_matmul",
    )(a, b)


def _new_token_mask(rows, t_new):
    tok = lax.broadcasted_iota(jnp.int32, (rows, LANES), 0) % t_new
    lane = lax.broadcasted_iota(jnp.int32, (rows, LANES), 1)
    return (lane < t_new) & (lane <= tok)


def _fetch_pages(pt_ref, cache_ref, buf, sem, pages):
    b = pl.program_id(0)
    slot = b % 2

    def copy(seq, p, s):
        return pltpu.make_async_copy(cache_ref.at[pt_ref[seq, p]], buf.at[s, p], sem.at[s])

    def start(seq, s):
        def body(p, c):
            copy(seq, p, s).start()
            return c
        lax.fori_loop(0, pages, body, 0)

    @pl.when(b == 0)
    def _():
        start(0, 0)

    @pl.when(b + 1 < pl.num_programs(0))
    def _():
        start(b + 1, 1 - slot)

    def wait(p, c):
        copy(b, p, slot).wait()
        return c
    lax.fori_loop(0, pages, wait, 0)
    return slot


def _page_scratch(pages, feat):
    return [pltpu.VMEM((2, pages, feat, PAGE_SIZE), F32), pltpu.SemaphoreType.DMA((2,))]


def _mla_sample_kernel(pt_ref, q_ref, new_ref, cache_ref, o_ref, kv_scr, new_scr, m_scr, l_scr, acc_scr,
                       page_buf, page_sem, *, pages, t_new):
    slot = _fetch_pages(pt_ref, cache_ref, page_buf, page_sem, pages)
    j = pl.program_id(1)
    width = kv_scr.shape[0]

    @pl.when((pl.program_id(0) == 0) & (j == 0))
    def _():
        kv_scr[...] = jnp.zeros(kv_scr.shape, BF16)

    @pl.when(j == 0)
    def _():
        m0, l0, a0 = _flash_init(q_ref.shape[1], width)
        m_scr[...], l_scr[...], acc_scr[...] = m0, l0, a0

    for p in range(pages):
        kv_scr[0:MLA_ROW, p * PAGE_SIZE:(p + 1) * PAGE_SIZE] = page_buf[slot, p].astype(BF16)
    q = q_ref[0]
    kv = kv_scr[...]
    carry = _online_update((m_scr[...], l_scr[...], acc_scr[...]), _dot(q, kv), lambda p: _dot_nt(p, kv))
    m_scr[...], l_scr[...], acc_scr[...] = carry

    @pl.when(j == pl.num_programs(1) - 1)
    def _():
        new_scr[...] = jnp.zeros(new_scr.shape, BF16)
        new_scr[0:t_new, 0:MLA_ROW] = new_ref[0].astype(BF16)
        kvn = new_scr[...]
        s = jnp.where(_new_token_mask(q.shape[0], t_new), _dot_nt(q, kvn), NEG_INF)
        _, l, acc = _online_update(carry, s, lambda p: _dot(p, kvn))
        o_ref[0] = acc / l


def _mla_sample(page_table, q, new_rows, cache_t, pages):
    nb, rows, width = q.shape
    t_new = new_rows.shape[1]
    n_pages = page_table.shape[1]
    kern = functools.partial(_mla_sample_kernel, pages=pages, t_new=t_new)
    grid_spec = pltpu.PrefetchScalarGridSpec(
        num_scalar_prefetch=1, grid=(nb, n_pages // pages),
        in_specs=[pl.BlockSpec((1, rows, width), lambda b, j, pt: (b, 0, 0)),
                  pl.BlockSpec((1, t_new, MLA_ROW), lambda b, j, pt: (b, 0, 0)), pl.BlockSpec(memory_space=pl.ANY)],
        out_specs=pl.BlockSpec((1, rows, width), lambda b, j, pt: (b, 0, 0)),
        scratch_shapes=[pltpu.VMEM((width, pages * PAGE_SIZE), BF16), pltpu.VMEM((LANES, width), BF16),
                        pltpu.VMEM((rows, 1), F32), pltpu.VMEM((rows, 1), F32), pltpu.VMEM((rows, width), F32)]
        + _page_scratch(pages, MLA_ROW))
    return pl.pallas_call(
        kern, grid_spec=grid_spec, out_shape=jax.ShapeDtypeStruct((nb, rows, width), F32),
        compiler_params=_params(("arbitrary", "arbitrary")), name="mla_sample",
    )(page_table, q, new_rows, cache_t)


def _cmp_pool_kernel(pt_ref, w_ref, e_ref, cache_ref, o_ref, hi_scr, lo_scr, page_buf, page_sem, *, pages):
    slot = _fetch_pages(pt_ref, cache_ref, page_buf, page_sem, pages)
    w = w_ref[...]
    for p in range(pages):
        x = page_buf[slot, p] * w
        hi = x.astype(BF16)
        sl = slice(p * PAGE_SIZE, (p + 1) * PAGE_SIZE)
        hi_scr[:, sl] = hi
        lo_scr[:, sl] = (x - hi.astype(F32)).astype(BF16)
    e = e_ref[...]
    o_ref[0] = _dot_nt(e, hi_scr[...]) + _dot_nt(e, lo_scr[...])


def _cmp_pool(page_table, w_pool_t, cache_t, pages):
    nb, n_pages = page_table.shape
    feat = cache_t.shape[1]
    per_step = pages * PAGE_SIZE // CMP_BLOCK
    expand = _block_expand(per_step, pages * PAGE_SIZE)
    kern = functools.partial(_cmp_pool_kernel, pages=pages)
    grid_spec = pltpu.PrefetchScalarGridSpec(
        num_scalar_prefetch=1, grid=(nb, n_pages // pages),
        in_specs=[pl.BlockSpec(w_pool_t.shape, lambda b, j, pt: (0, 0)),
                  pl.BlockSpec(expand.shape, lambda b, j, pt: (0, 0)), pl.BlockSpec(memory_space=pl.ANY)],
        out_specs=pl.BlockSpec((1, per_step, feat), lambda b, j, pt: (b, j, 0)),
        scratch_shapes=[pltpu.VMEM((feat, pages * PAGE_SIZE), BF16), pltpu.VMEM((feat, pages * PAGE_SIZE), BF16)]
        + _page_scratch(pages, feat))
    return pl.pallas_call(
        kern, grid_spec=grid_spec,
        out_shape=jax.ShapeDtypeStruct((nb, n_pages * PAGE_SIZE // CMP_BLOCK, feat), F32),
        compiler_params=_params(("arbitrary", "arbitrary")), name="cmp_pool",
    )(page_table, w_pool_t, expand, cache_t)


def _pack_group(rows, g):
    ko, vo = g * NSA_DIM, NSA_GROUPS * NSA_DIM + g * NSA_DIM
    return jnp.concatenate([rows[:, ko:ko + NSA_DIM], rows[:, vo:vo + NSA_DIM]], axis=1).astype(BF16)


def _pack_group_t(feats, g):
    ko, vo = g * NSA_DIM, NSA_GROUPS * NSA_DIM + g * NSA_DIM
    return jnp.concatenate([feats[ko:ko + NSA_DIM], feats[vo:vo + NSA_DIM]], axis=0).astype(BF16)


def _nsa_sample_kernel(pt_ref, q_ref, pool_ref, gate_ref, win_ref, wnew_ref, snew_ref, expand_ref, cache_ref,
                       o_ref, kv_scr, new_scr, q_scr, sel_scr, oc_scr, ow_scr, m_scr, l_scr, acc_scr,
                       page_buf, page_sem, *, pages, t_new):
    slot = _fetch_pages(pt_ref, cache_ref, page_buf, page_sem, pages)
    j = pl.program_id(1)
    rep = NSA_REP
    rows = rep * t_new
    nblk = pool_ref.shape[1]
    tk = pages * PAGE_SIZE
    new_ok = _new_token_mask(rows, t_new)

    def new_tile(ref, g):
        new_scr[...] = jnp.zeros(new_scr.shape, BF16)
        new_scr[0:t_new, :] = _pack_group(ref[0], g)
        return new_scr[...]

    @pl.when(j == 0)
    def _():
        pool = pool_ref[0]
        win_t = win_ref[0]
        wb = win_t.shape[1]
        tok = lax.broadcasted_iota(jnp.int32, (rows, 1), 0) % t_new
        lane = lax.broadcasted_iota(jnp.int32, (nblk, LANES), 1)
        blk = lax.broadcasted_iota(jnp.int32, (nblk, LANES), 0)
        for g in range(NSA_GROUPS):
            q = q_ref[0, g * rows:(g + 1) * rows]
            pg = _pack_group(pool, g)
            oc_scr[g] = _dot(_softmax(_dot_nt(q, pg), axis=1).astype(BF16), pg)
            q_scr[...] = jnp.zeros(q_scr.shape, BF16)
            q_scr[0:rows, :] = q
            p_t = _softmax(_dot_nt(pg, q_scr[...]), axis=0)
            tot = p_t
            for r in range(1, rep):
                tot = tot + pltpu.roll(p_t, LANES - r * t_new, 1)
            tot = jnp.where(lane < t_new, tot, 0.0)
            imp = tot
            for r in range(1, rep):
                imp = imp + pltpu.roll(tot, r * t_new, 1)
            forced = (blk == 0) | (blk == nblk - 1)
            sel = forced | _top_blocks(jnp.where(forced, -jnp.inf, imp), SEL_TOPK - 3)
            sel_scr[g] = jnp.where(sel, 1.0, 0.0).T[0:rows]
            wg = _pack_group_t(win_t, g)
            i = lax.broadcasted_iota(jnp.int32, (1, wb), 1)
            ng = new_tile(wnew_ref, g)
            s_w = jnp.concatenate([jnp.where(i > tok + (wb - WINDOW), _dot(q, wg), NEG_INF),
                                   jnp.where(new_ok, _dot_nt(q, ng), NEG_INF)], axis=1)
            p_w = _softmax(s_w, axis=1).astype(BF16)
            ow_scr[g] = _dot_nt(p_w[:, :wb], wg) + _dot(p_w[:, wb:], ng)
            m0, l0, a0 = _flash_init(rows, LANES)
            m_scr[g], l_scr[g], acc_scr[g] = m0, l0, a0

    for p in range(pages):
        page = page_buf[slot, p]
        for g in range(NSA_GROUPS):
            kv_scr[g, :, p * PAGE_SIZE:(p + 1) * PAGE_SIZE] = _pack_group_t(page, g)
    sl = pl.ds(pl.multiple_of(j * tk, tk), tk)
    for g in range(NSA_GROUPS):
        q = q_ref[0, g * rows:(g + 1) * rows]
        kv = kv_scr[g]
        hit = _dot(sel_scr[g].astype(BF16), expand_ref[:, sl])
        s = _dot(q, kv) + (hit - 1.0) * (-NEG_INF)
        carry = _online_update((m_scr[g], l_scr[g], acc_scr[g]), s, lambda p, kv=kv: _dot_nt(p, kv))
        m_scr[g], l_scr[g], acc_scr[g] = carry

    @pl.when(j == pl.num_programs(1) - 1)
    def _():
        gates = gate_ref[0]
        for g in range(NSA_GROUPS):
            q = q_ref[0, g * rows:(g + 1) * rows]
            ng = new_tile(snew_ref, g)
            s = jnp.where(new_ok, _dot_nt(q, ng), NEG_INF)
            _, l, acc = _online_update((m_scr[g], l_scr[g], acc_scr[g]), s, lambda p, ng=ng: _dot(p, ng))
            gg = gates[g * rows:(g + 1) * rows]
            o_ref[0, g * rows:(g + 1) * rows, :] = (gg[:, 0:1] * oc_scr[g] + gg[:, 1:2] * (acc / l)
                                                    + gg[:, 2:3] * ow_scr[g])


def _nsa_sample(page_table, q, pool, gates, win_t, win_new, slc_new, expand, cache_t, pages):
    nb, rows2, _ = q.shape
    t_new = slc_new.shape[1]
    n_pages = page_table.shape[1]
    feat = cache_t.shape[1]
    rows = rows2 // NSA_GROUPS
    per_b = lambda shape: pl.BlockSpec((1,) + shape, lambda b, j, pt: (b, 0, 0))
    kern = functools.partial(_nsa_sample_kernel, pages=pages, t_new=t_new)
    grp = lambda *shape: pltpu.VMEM((NSA_GROUPS,) + shape, F32)
    grid_spec = pltpu.PrefetchScalarGridSpec(
        num_scalar_prefetch=1, grid=(nb, n_pages // pages),
        in_specs=[per_b((rows2, LANES)), per_b(pool.shape[1:]), per_b((rows2, LANES)), per_b(win_t.shape[1:]),
                  per_b((t_new, feat)), per_b((t_new, feat)),
                  pl.BlockSpec(expand.shape, lambda b, j, pt: (0, 0)), pl.BlockSpec(memory_space=pl.ANY)],
        out_specs=per_b((rows2, LANES)),
        scratch_shapes=[pltpu.VMEM((NSA_GROUPS, LANES, pages * PAGE_SIZE), BF16), pltpu.VMEM((LANES, LANES), BF16),
                        pltpu.VMEM((LANES, LANES), BF16), grp(rows, expand.shape[0]),
                        grp(rows, LANES), grp(rows, LANES), grp(rows, 1), grp(rows, 1), grp(rows, LANES)]
        + _page_scratch(pages, feat))
    return pl.pallas_call(
        kern, grid_spec=grid_spec, out_shape=jax.ShapeDtypeStruct((nb, rows2, LANES), F32),
        compiler_params=_params(("arbitrary", "arbitrary")), name="nsa_sample",
    )(page_table, q, pool, gates, win_t, win_new, slc_new, expand, cache_t)


def _tail_kernel(x_ref, ya_ref, yb_ref, ga_ref, gb_ref, p_ref, wba_ref, wbn_ref, wo_ref, lnf_ref,
                 wg_ref, wu_ref, wd_ref, lnp_ref, wpg_ref, wpp_ref, lnl_ref, o_ref, hid_scr, *, chunk):
    merged = ga_ref[...] * _dot(ya_ref[...], wba_ref[...]) + gb_ref[...] * _dot(yb_ref[...], wbn_ref[...])
    x = x_ref[...] + _dot(merged.astype(BF16), wo_ref[...])
    h = _rms(x, lnf_ref[...]).astype(BF16)

    def ffn(c, carry):
        sl = pl.ds(pl.multiple_of(c * chunk, chunk), chunk)
        hid_scr[:, sl] = (jax.nn.silu(_dot(h, wg_ref[:, sl])) * _dot(h, wu_ref[:, sl])).astype(BF16)
        return carry

    lax.fori_loop(0, wg_ref.shape[1] // chunk, ffn, 0)
    x = x + _dot(hid_scr[...], wd_ref[...])
    gate = jax.nn.sigmoid(_dot(_rms(x, lnp_ref[...]).astype(BF16), wpg_ref[...]))
    x = x + gate * _dot(p_ref[...].astype(BF16), wpp_ref[...])
    o_ref[...] = _rms(x, lnl_ref[...])


def _tail(x, ya, yb, ga, gb, p, weights, tm):
    n, d_model = x.shape
    row = lambda a: pl.BlockSpec((tm, a.shape[1]), lambda i: (i, 0))
    acts = (x, ya, yb, ga, gb, p)
    kern = functools.partial(_tail_kernel, chunk=2 * LANES)
    return pl.pallas_call(
        kern, grid=(n // tm,),
        in_specs=[row(a) for a in acts] + [_resident(w.shape) for w in weights],
        out_specs=pl.BlockSpec((tm, d_model), lambda i: (i, 0)),
        out_shape=jax.ShapeDtypeStruct((n, d_model), F32),
        scratch_shapes=[pltpu.VMEM((tm, weights[4].shape[1]), BF16)],
        compiler_params=_params(("arbitrary",)), name="tail",
    )(*acts, *weights)


def _rope_tables(pos):
    pos = pos.astype(F32)[:, None]
    n = pos.shape[0]

    def cs(rot):
        half = rot // 2
        inv = jnp.float32(ROPE_THETA) ** (-jnp.arange(half, dtype=F32) * 2.0 / rot)
        ang = pos * inv[None, :]
        return jnp.cos(ang), jnp.sin(ang)

    one = lambda w: jnp.ones((n, w), F32)
    zero = lambda w: jnp.zeros((n, w), F32)
    c, s = cs(MLA_ROPE)
    pad = LANES - MLA_NOPE - MLA_ROPE
    cm = jnp.concatenate([one(MLA_NOPE), c, c, one(pad)], axis=1)
    sm = jnp.concatenate([zero(MLA_NOPE), -s, s, zero(pad)], axis=1)
    c, s = cs(NSA_ROT)
    rest = NSA_DIM - NSA_ROT
    cn = jnp.concatenate([c, c, one(rest)] * (LANES // NSA_DIM), axis=1)
    sn = jnp.concatenate([-s, s, zero(rest)] * (LANES // NSA_DIM), axis=1)
    return cm, sm, cn, sn


def _layout_w_in(w_in):
    d_model = w_in.shape[0]
    sizes = (MLA_HEADS * (MLA_NOPE + MLA_ROPE), MLA_KV_LORA, MLA_ROPE, NSA_HEADS * NSA_DIM,
             SEG_KV // 3, SEG_KV // 3, SEG_KV // 3, 3 * NSA_HEADS, 2 * d_model)
    offs = np.concatenate([[0], np.cumsum(sizes)])
    qa, ca, ra, qb, zc, zs, zw, gn, gm = [w_in[:, int(offs[i]):int(offs[i + 1])] for i in range(len(sizes))]
    qa = qa.reshape(d_model, MLA_HEADS, MLA_NOPE + MLA_ROPE)
    qa = jnp.pad(qa, ((0, 0), (0, 0), (0, LANES - MLA_NOPE - MLA_ROPE))).reshape(d_model, SEG_QA)
    ra = jnp.pad(ra, ((0, 0), (MLA_NOPE, LANES - MLA_NOPE - MLA_ROPE)))
    gn = jnp.pad(gn, ((0, 0), (0, SEG_GN - gn.shape[1])))
    return jnp.concatenate([qa, ca, ra, qb, zc, zs, zw, gn, gm], axis=1).astype(BF16)


def _layout_w_up(w_uk, w_uv):
    wk = jnp.pad(w_uk, ((0, 0), (0, 0), (0, LANES - MLA_NOPE))).reshape(MLA_KV_LORA, SEG_QA)
    return jnp.concatenate([wk, w_uv.reshape(MLA_KV_LORA, MLA_HEADS * MLA_V)], axis=1).astype(BF16)


def _layout_absorb(w_uk, w_uv, width):
    to_lat = jnp.zeros((MLA_HEADS, LANES, width), F32)
    to_lat = to_lat.at[:, 0:MLA_NOPE, 0:MLA_KV_LORA].set(w_uk.transpose(1, 2, 0))
    to_lat = to_lat.at[:, MLA_NOPE:MLA_NOPE + MLA_ROPE, MLA_KV_LORA:MLA_ROW].set(jnp.eye(MLA_ROPE, dtype=F32))
    to_val = jnp.zeros((MLA_HEADS, width, MLA_V), F32).at[:, 0:MLA_KV_LORA, :].set(w_uv.transpose(1, 0, 2))
    return to_lat.astype(BF16), to_val.astype(BF16)


def _block_expand(nblk, nkeys):
    return (jnp.arange(nkeys)[None, :] // CMP_BLOCK == jnp.arange(nblk)[:, None]).astype(BF16)


def kernel(x_prompt, x_sample, cache_mla, cache_nsa_cmp, cache_nsa_slc, state_nsa_win, page_table, p_prompt, p_sample, ln_attn, w_in, mla_kv_norm, mla_w_uk, mla_w_uv, nsa_w_cmp_k, nsa_w_cmp_v, w_branch_mla, w_branch_nsa, w_out, ln_ffn, w_ffn_gate, w_ffn_up, w_ffn_down, ln_ple, w_ple_gate, w_ple_proj, ln_final):
    nb, ns, d_model = x_prompt.shape
    db, t_new, _ = x_sample.shape
    depth = w_in.shape[0]
    n_pages = page_table.shape[1]
    past = n_pages * PAGE_SIZE
    wb = state_nsa_win.shape[2]
    width = 2 * NSA_GROUPS * NSA_DIM
    g_, d_ = NSA_GROUPS, NSA_DIM
    assert depth == 1 and t_new <= CMP_BLOCK and past % CMP_BLOCK == 0 and wb == WINDOW and past >= wb
    assert ns % 512 == 0 and (db * t_new) % 512 == 0 and NSA_REP * t_new <= LANES
    tm = 512
    pages = n_pages
    assert pages <= 64
    lat_w = 3 * LANES
    i = 0

    w_all = _layout_w_in(w_in[i])
    w_up = _layout_w_up(mla_w_uk[i], mla_w_uv[i])
    to_lat, to_val = _layout_absorb(mla_w_uk[i], mla_w_uv[i], lat_w)
    w_pool = jnp.concatenate([nsa_w_cmp_k[i]] * g_ + [nsa_w_cmp_v[i]] * g_, axis=1)
    w_pool_t = jnp.tile(w_pool.T, (1, PAGE_SIZE // CMP_BLOCK))
    ln_a = ln_attn[i][None, :]
    kvn = mla_kv_norm[i][None, :]
    tail_w = (w_branch_mla[i].astype(BF16), w_branch_nsa[i].astype(BF16), w_out[i].astype(BF16),
              ln_ffn[i][None, :], w_ffn_gate[i].astype(BF16), w_ffn_up[i].astype(BF16),
              w_ffn_down[i].astype(BF16), ln_ple[i][None, :], w_ple_gate[i].astype(BF16),
              w_ple_proj[i].astype(BF16), ln_final[None, :])

    xp = x_prompt.reshape(nb * ns, d_model)
    nblk = ns // CMP_BLOCK
    (qm, km, vm, mla_p, qn, kvc_p, kvs_p, kvw_p, ka, va, win, pool, gn, ga, gb) = _proj(
        xp, _rope_tables(jnp.arange(ns)), ln_a, w_all, w_up, kvn, w_pool, nb, ns, tm, nblk)
    token_major = lambda a: a.transpose(0, 2, 1).reshape(-1, a.shape[1])
    mla_p, kvc_p, kvs_p, kvw_p = (token_major(a) for a in (mla_p, kvc_p, kvs_p, kvw_p))
    ya = _mla_flash(qm, km, vm, 512)
    pool5 = pool.reshape(nb, nblk, 2, g_, d_)
    pool_g = jnp.concatenate([pool5[:, :, 0].transpose(0, 2, 1, 3), pool5[:, :, 1].transpose(0, 2, 1, 3)], axis=-1)
    yb = _nsa_prompt(qn, ka, va, win, pool_g, gn.reshape(nb, ns, LANES), 128, 512)
    y_prompt = _tail(xp, ya.reshape(nb * ns, -1), yb.reshape(nb * ns, -1), ga, gb,
                     p_prompt[i].reshape(nb * ns, -1), tail_w, tm).reshape(nb, ns, d_model)

    n_s = db * t_new
    xs = x_sample.reshape(n_s, d_model)
    pos_s = jnp.tile(past + jnp.arange(t_new), db)
    (qm, _, _, mla_s, qn, kvc_s, kvs_s, kvw_s, _, _, _, _, gn, ga, gb) = _proj(
        xs, _rope_tables(pos_s), ln_a, w_all, w_up, kvn, w_pool, 1, n_s, tm, LANES)
    mla_s, kvc_s, kvs_s, kvw_s = (token_major(a) for a in (mla_s, kvc_s, kvs_s, kvw_s))
    by_batch = lambda a: a.reshape(a.shape[0], db, t_new, a.shape[-1]).transpose(1, 0, 2, 3).reshape(
        db, a.shape[0] * t_new, a.shape[-1])
    q_lat = by_batch(_head_matmul(qm[0], to_lat, BF16))
    mla_t = cache_mla[i].transpose(0, 2, 1)
    o_lat = _mla_sample(page_table, q_lat, mla_s.reshape(db, t_new, MLA_ROW), mla_t, pages)
    o_lat = o_lat.reshape(db, MLA_HEADS, t_new, lat_w).transpose(1, 0, 2, 3).reshape(MLA_HEADS, n_s, lat_w)
    ya = _head_matmul(o_lat.astype(BF16), to_val, BF16).transpose(1, 0, 2).reshape(n_s, MLA_HEADS * MLA_V)

    cmp_t = cache_nsa_cmp[i].reshape(-1, PAGE_SIZE, width).transpose(0, 2, 1)
    slc_t = cache_nsa_slc[i].reshape(-1, PAGE_SIZE, width).transpose(0, 2, 1)
    pool_s = _cmp_pool(page_table, w_pool_t, cmp_t, pages)
    gate_s = gn[:, :3 * NSA_HEADS].reshape(db, t_new, NSA_HEADS, 3).transpose(0, 2, 1, 3).reshape(
        db, NSA_HEADS * t_new, 3)
    gate_s = jnp.pad(gate_s, ((0, 0), (0, 0), (0, LANES - 3)))
    win_state = state_nsa_win[i].reshape(db, wb, width)
    o_nsa = _nsa_sample(page_table, by_batch(qn[0]), pool_s, gate_s, win_state.transpose(0, 2, 1),
                        kvw_s.reshape(db, t_new, width), kvs_s.reshape(db, t_new, width),
                        _block_expand(past // CMP_BLOCK, past), slc_t, pages)
    yb = o_nsa[:, :, NSA_DIM:].reshape(db, NSA_HEADS, t_new, d_).transpose(0, 2, 1, 3).reshape(n_s, -1)
    y_sample = _tail(xs, ya, yb.astype(BF16), ga, gb, p_sample[i].reshape(n_s, -1), tail_w, tm).reshape(
        db, t_new, d_model)

    kv5 = lambda a, b, s: a.reshape(1, b, s, 2, g_, d_)
    win_all = jnp.concatenate([win_state, kvw_s.reshape(db, t_new, width)], axis=1)
    wkeep = min(WINDOW, ns)
    return (y_prompt, y_sample,
            mla_p.reshape(1, nb, ns, MLA_ROW), mla_s.reshape(1, db, t_new, MLA_ROW),
            kv5(kvc_p, nb, ns), kv5(kvc_s, db, t_new), kv5(kvs_p, nb, ns), kv5(kvs_s, db, t_new),
            kv5(kvw_p, nb, ns)[:, :, ns - wkeep:],
            kv5(win_all[:, -min(WINDOW, past + t_new):], db, min(WINDOW, past + t_new)))
```

```python
import functools
import math

import jax
import jax.numpy as jnp
import numpy as np
from jax import lax
from jax.experimental import pallas as pl
from jax.experimental.pallas import tpu as pltpu

F32 = jnp.float32
BF16 = jnp.bfloat16

NORM_EPS = 1e-6
ROPE_THETA = 500000.0
NEG_INF = -1e30
LOG2E = math.log2(math.e)
PAGE_SIZE = 128
MLA_HEADS = 8
MLA_NOPE = 64
MLA_ROPE = 32
MLA_V = 64
MLA_KV_LORA = 256
MLA_ROW = MLA_KV_LORA + MLA_ROPE
NSA_HEADS = 8
NSA_GROUPS = 2
NSA_REP = NSA_HEADS // NSA_GROUPS
NSA_DIM = 64
NSA_ROT = NSA_DIM // 4
CMP_BLOCK = 64
SEL_TOPK = 16
WINDOW = 512
LANES = 128
VMEM_LIMIT = 60 * 1024 * 1024

SEG_QA = MLA_HEADS * LANES
SEG_C = MLA_KV_LORA
SEG_R = LANES
SEG_QB = NSA_HEADS * NSA_DIM
SEG_KV = 3 * 2 * NSA_GROUPS * NSA_DIM
SEG_GN = LANES
OFF_C = SEG_QA
OFF_R = OFF_C + SEG_C
OFF_QB = OFF_R + SEG_R
OFF_KV = OFF_QB + SEG_QB
OFF_GN = OFF_KV + SEG_KV
OFF_GM = OFF_GN + SEG_GN


def _params(sem):
    return pltpu.CompilerParams(dimension_semantics=sem, vmem_limit_bytes=VMEM_LIMIT)


def _resident(shape):
    nd = len(shape)
    return pl.BlockSpec(shape, lambda *_: (0,) * nd, pipeline_mode=pl.Buffered(1))


def _dot(a, b):
    return jnp.dot(a, b, preferred_element_type=F32)


def _dot_nt(a, b):
    return lax.dot_general(a, b, (((1,), (1,)), ((), ())), preferred_element_type=F32)


def _rms(x, g):
    return x * lax.rsqrt(jnp.mean(x * x, axis=-1, keepdims=True) + NORM_EPS) * g


def _value_slot(v, rows):
    lane = lax.broadcasted_iota(jnp.int32, (rows, LANES - v.shape[1]), 1)
    return jnp.concatenate([v, jnp.where(lane == 0, 1.0, 0.0)], axis=1).astype(BF16)


def _proj_kernel(x_ref, ln_ref, w_ref, wup_ref, kvn_ref, cm_ref, sm_ref, cn_ref, sn_ref, wpool_ref,
                 qm_ref, km_ref, vm_ref, mla_ref, qn_ref, kvc_ref, kvs_ref, kvw_ref,
                 ka_ref, va_ref, win_ref, pool_ref, gn_ref, ga_ref, gb_ref, *, nt):
    tm = x_ref.shape[0]
    d_model = x_ref.shape[1]
    nblk = ka_ref.shape[3] - LANES
    h = _rms(x_ref[...], ln_ref[...]).astype(BF16)
    lane = lax.broadcasted_iota(jnp.int32, (tm, LANES), 1)
    cm, sm, cn, sn = cm_ref[...], sm_ref[...], cn_ref[...], sn_ref[...]

    def rope_m(z):
        half = MLA_ROPE // 2
        sw = jnp.where(lane < MLA_NOPE + half, pltpu.roll(z, LANES - half, 1), pltpu.roll(z, half, 1))
        return z * cm + sw * sm

    def rope_n(z):
        half = NSA_ROT // 2
        sw = jnp.where((lane & (NSA_DIM - 1)) < half, pltpu.roll(z, LANES - half, 1), pltpu.roll(z, half, 1))
        return z * cn + sw * sn

    zq = _dot(h, w_ref[:, 0:SEG_QA])
    scale_m = (MLA_NOPE + MLA_ROPE) ** -0.5 * LOG2E
    for hh in range(MLA_HEADS):
        qm_ref[0, hh] = (rope_m(zq[:, hh * LANES:(hh + 1) * LANES]) * scale_m).astype(BF16)

    zc = _dot(h, w_ref[:, OFF_C:OFF_R])
    c = _rms(zc, kvn_ref[...])
    kpe = rope_m(_dot(h, w_ref[:, OFF_R:OFF_QB]))
    mla_ref[0, 0:MLA_KV_LORA, :] = c.T
    mla_ref[0, MLA_KV_LORA:MLA_ROW, :] = kpe.T[MLA_NOPE:MLA_NOPE + MLA_ROPE]
    up = _dot(c.astype(BF16), wup_ref[...])
    for hh in range(MLA_HEADS):
        km_ref[0, hh] = (up[:, hh * LANES:(hh + 1) * LANES] + kpe).astype(BF16)
        vo = SEG_QA + hh * MLA_V
        vm_ref[0, hh] = _value_slot(up[:, vo:vo + MLA_V], tm)

    zqb = _dot(h, w_ref[:, OFF_QB:OFF_KV])
    zero_half = jnp.zeros((tm, NSA_DIM), F32)
    scale_n = NSA_DIM ** -0.5 * LOG2E
    for pair in range(NSA_HEADS // 2):
        zz = rope_n(zqb[:, pair * LANES:(pair + 1) * LANES]) * scale_n
        qn_ref[0, 2 * pair] = jnp.concatenate([zz[:, :NSA_DIM], zero_half], axis=1).astype(BF16)
        qn_ref[0, 2 * pair + 1] = jnp.concatenate([zz[:, NSA_DIM:], zero_half], axis=1).astype(BF16)

    zkv = _dot(h, w_ref[:, OFF_KV:OFF_GN])
    width = 2 * NSA_GROUPS * NSA_DIM

    def kv_rows(i):
        z = zkv[:, i * width:(i + 1) * width]
        return jnp.concatenate([rope_n(z[:, :LANES]), z[:, LANES:]], axis=1)

    rows_c, rows_s, rows_w = kv_rows(0), kv_rows(1), kv_rows(2)
    kvc_ref[0] = rows_c.T
    kvs_ref[0] = rows_s.T
    kvw_ref[0] = rows_w.T
    pos = (pl.program_id(0) % nt) * tm + lax.broadcasted_iota(jnp.int32, (tm, nblk), 0)
    onehot = jnp.where(lax.broadcasted_iota(jnp.int32, (tm, nblk), 1) == pos // CMP_BLOCK, 1.0, 0.0)
    for g in range(NSA_GROUPS):
        ko, vo = g * NSA_DIM, LANES + g * NSA_DIM
        ka_ref[0, g] = jnp.concatenate([rows_s[:, ko:ko + NSA_DIM], zero_half, onehot], axis=1).astype(BF16)
        unit = jnp.where(lax.broadcasted_iota(jnp.int32, (tm, NSA_DIM), 1) == 0, 1.0, 0.0)
        va_ref[0, g] = jnp.concatenate([unit, rows_s[:, vo:vo + NSA_DIM]], axis=1).astype(BF16)
        win_ref[0, g] = jnp.concatenate([rows_w[:, ko:ko + NSA_DIM], rows_w[:, vo:vo + NSA_DIM]],
                                        axis=1).astype(BF16)
    pool_ref[...] = jnp.sum(rows_c.reshape(tm // CMP_BLOCK, CMP_BLOCK, width) * wpool_ref[...][None], axis=1)

    gn_ref[...] = jax.nn.sigmoid(_dot(h, w_ref[:, OFF_GN:OFF_GM]))
    zm = _dot(h, w_ref[:, OFF_GM:OFF_GM + 2 * d_model])
    ga_ref[...] = jax.nn.sigmoid(zm[:, :d_model])
    gb_ref[...] = jax.nn.sigmoid(zm[:, d_model:])


def _proj(x, tabs, ln, w_all, w_up, kv_norm, w_pool, nb, ns, tm, nblk):
    n, d_model = x.shape
    nt = ns // tm
    width = 2 * NSA_GROUPS * NSA_DIM
    row = lambda w: pl.BlockSpec((tm, w), lambda i: (i, 0))
    head = lambda hn, w: pl.BlockSpec((1, hn, tm, w), lambda i: (i // nt, 0, i % nt, 0))
    tab = pl.BlockSpec((tm, LANES), lambda i: (i % nt, 0))
    feat = lambda f: pl.BlockSpec((1, f, tm), lambda i: (i // nt, 0, i % nt))
    slots = (
        (head(MLA_HEADS, LANES), (nb, MLA_HEADS, ns, LANES), BF16),
        (head(MLA_HEADS, LANES), (nb, MLA_HEADS, ns, LANES), BF16),
        (head(MLA_HEADS, LANES), (nb, MLA_HEADS, ns, LANES), BF16),
        (feat(MLA_ROW), (nb, MLA_ROW, ns), F32),
        (head(NSA_HEADS, LANES), (nb, NSA_HEADS, ns, LANES), BF16),
        (feat(width), (nb, width, ns), F32),
        (feat(width), (nb, width, ns), F32),
        (feat(width), (nb, width, ns), F32),
        (head(NSA_GROUPS, LANES + nblk), (nb, NSA_GROUPS, ns, LANES + nblk), BF16),
        (head(NSA_GROUPS, LANES), (nb, NSA_GROUPS, ns, LANES), BF16),
        (head(NSA_GROUPS, LANES), (nb, NSA_GROUPS, ns, LANES), BF16),
        (pl.BlockSpec((tm // CMP_BLOCK, width), lambda i: (i, 0)), (n // CMP_BLOCK, width), F32),
        (row(LANES), (n, LANES), F32),
        (row(d_model), (n, d_model), F32),
        (row(d_model), (n, d_model), F32),
    )
    in_specs = [row(d_model), _resident(ln.shape), _resident(w_all.shape), _resident(w_up.shape),
                _resident(kv_norm.shape), tab, tab, tab, tab, _resident(w_pool.shape)]
    return pl.pallas_call(
        functools.partial(_proj_kernel, nt=nt), grid=(n // tm,), in_specs=in_specs,
        out_specs=tuple(s[0] for s in slots),
        out_shape=tuple(jax.ShapeDtypeStruct(s[1], s[2]) for s in slots),
        compiler_params=_params(("arbitrary",)), name="proj",
    )(x, ln, w_all, w_up, kv_norm, *tabs, w_pool)


def _flash_update(m, acc, s, v):
    m_new = jnp.maximum(m, jnp.max(s, axis=-1, keepdims=True))
    alpha = jnp.exp2(m - m_new)
    p = jnp.exp2(s - m_new).astype(BF16)
    return m_new, alpha * acc + _dot(p, v)


def _flash_finish(acc, dv):
    return acc[:, :dv] / acc[:, dv:dv + 1]


def _online_update(carry, s, pv):
    m, l, acc = carry
    m_new = jnp.maximum(m, jnp.max(s, axis=-1, keepdims=True))
    alpha = jnp.exp2(m - m_new)
    p = jnp.exp2(s - m_new)
    l = alpha * l + jnp.sum(p, axis=-1, keepdims=True)
    acc = alpha * acc + pv(p.astype(BF16))
    return m_new, l, acc


def _flash_init(rows, dv):
    return (jnp.full((rows, 1), NEG_INF, F32), jnp.zeros((rows, 1), F32), jnp.zeros((rows, dv), F32))


def _mla_flash_kernel(q_ref, k_ref, v_ref, o_ref, *, t):
    qi = pl.program_id(2)
    heads = q_ref.shape[1]
    qs = [q_ref[0, hh] for hh in range(heads)]
    causal = lax.broadcasted_iota(jnp.int32, (t, t), 1) <= lax.broadcasted_iota(jnp.int32, (t, t), 0)

    def step(j, carry, mask):
        sl = pl.ds(pl.multiple_of(j * t, t), t)
        scores = [_dot_nt(qs[hh], k_ref[0, hh, sl, :]) for hh in range(heads)]
        if mask is not None:
            scores = [jnp.where(mask, s, NEG_INF) for s in scores]
        return tuple(_flash_update(*carry[hh], scores[hh], v_ref[0, hh, sl, :]) for hh in range(heads))

    init = tuple((jnp.full((t, 1), NEG_INF, F32), jnp.zeros((t, LANES), F32)) for _ in range(heads))
    carry = lax.fori_loop(0, qi, lambda j, c: step(j, c, None), init)
    carry = step(qi, carry, causal)
    o_ref[0] = jnp.concatenate([_flash_finish(acc, MLA_V) for _, acc in carry], axis=1).astype(o_ref.dtype)


def _mla_flash(q, k, v, t):
    nb, nh, ns, _ = q.shape
    hp = 8
    kern = functools.partial(_mla_flash_kernel, t=t)
    whole = pl.BlockSpec((1, hp, ns, LANES), lambda b, h, i: (b, h, 0, 0), pipeline_mode=pl.Buffered(1))
    return pl.pallas_call(
        kern, grid=(nb, nh // hp, ns // t),
        in_specs=[pl.BlockSpec((1, hp, t, LANES), lambda b, h, i: (b, h, i, 0)), whole, whole],
        out_specs=pl.BlockSpec((1, t, hp * MLA_V), lambda b, h, i: (b, i, h)),
        out_shape=jax.ShapeDtypeStruct((nb, ns, nh * MLA_V), BF16),
        compiler_params=_params(("arbitrary", "arbitrary", "arbitrary")), name="mla_flash",
    )(q, k, v)


def _masked_softmax(s, mask, axis):
    s = jnp.where(mask, s, NEG_INF)
    m = jnp.max(s, axis=axis, keepdims=True)
    e = jnp.where(mask, jnp.exp2(s - m), 0.0)
    return e / jnp.maximum(jnp.sum(e, axis=axis, keepdims=True), 1e-30)


def _softmax(s, axis):
    e = jnp.exp2(s - jnp.max(s, axis=axis, keepdims=True))
    return e / jnp.sum(e, axis=axis, keepdims=True)


def _top_blocks(imp, rounds):
    blk = lax.broadcasted_iota(jnp.int32, imp.shape, 0)
    sel = jnp.zeros(imp.shape, jnp.bool_)
    v = imp
    for _ in range(rounds):
        m = jnp.max(v, axis=0, keepdims=True)
        first = jnp.min(jnp.where(v == m, blk, imp.shape[0]), axis=0, keepdims=True)
        hit = blk == first
        sel = sel | hit
        v = jnp.where(hit, -jnp.inf, v)
    return sel


def _tile_heads(x, rep):
    return jnp.concatenate([x] * rep, axis=0)


def _nsa_prompt_kernel(q_ref, ka_ref, va_ref, win_ref, pool_ref, gate_ref, o_ref, *, tq, tk):
    qi = pl.program_id(1)
    nblk = pool_ref.shape[2]
    rep = NSA_REP
    rows = rep * tq
    start = qi * tq
    qpos4 = start + (lax.broadcasted_iota(jnp.int32, (rows, 1), 0) & (tq - 1))
    blk4 = lax.broadcasted_iota(jnp.int32, (rows, nblk), 1)
    blk_t = lax.broadcasted_iota(jnp.int32, (nblk, tq), 0)
    cur_t = (start + lax.broadcasted_iota(jnp.int32, (nblk, tq), 1)) // CMP_BLOCK
    blk_t4 = lax.broadcasted_iota(jnp.int32, (nblk, rows), 0)
    qpos_t4 = start + (lax.broadcasted_iota(jnp.int32, (nblk, rows), 1) & (tq - 1))
    forced = (blk_t == 0) | (blk_t == cur_t) | (blk_t == cur_t - 1)
    seen = blk_t <= cur_t
    own = (blk_t >= start // CMP_BLOCK) & (blk_t < (start + tq) // CMP_BLOCK)
    tri = (lax.broadcasted_iota(jnp.int32, (rows, tq), 1)
           <= (lax.broadcasted_iota(jnp.int32, (rows, tq), 0) & (tq - 1)))
    gates = gate_ref[0]

    ntile = WINDOW // tq + 1
    qs, pools, s_cmp, s_blk, s_win, win_tiles = [], [], [], [], [], []
    for g in range(NSA_GROUPS):
        q = q_ref[0, g * rep:(g + 1) * rep].reshape(rows, LANES)
        pool = pool_ref[0, g].astype(BF16)
        qs.append(q)
        pools.append(pool)
        s_cmp.append(_dot_nt(q, pool))
        s_blk.append(_dot_nt(pool, q))
        scores, tiles = [], []
        for c in range(ntile):
            idx = qi - (ntile - 1) + c
            sl = pl.ds(pl.multiple_of(jnp.maximum(idx, 0) * tq, tq), tq)
            kvw = win_ref[0, g, sl, :]
            s = _dot_nt(q, kvw)
            if c == ntile - 1:
                s = jnp.where(tri, s, NEG_INF)
            else:
                if c == 0:
                    s = jnp.where(tri, NEG_INF, s)
                s = s + jnp.where(idx >= 0, 0.0, NEG_INF)
            scores.append(s)
            tiles.append(kvw)
        s_win.append(jnp.concatenate(scores, axis=1))
        win_tiles.append(tiles)

    q_aug, o_cmp, o_win = [], [], []
    for g in range(NSA_GROUPS):
        q, pool, tiles = qs[g], pools[g], win_tiles[g]
        p_c = _masked_softmax(s_cmp[g], (blk4 + 1) * CMP_BLOCK - 1 <= qpos4, axis=1)
        o_cmp.append(_dot(p_c.astype(BF16), pool))
        p_t = _masked_softmax(s_blk[g], (blk_t4 + 1) * CMP_BLOCK - 1 <= qpos_t4, axis=0)
        imp = p_t[:, 0:tq]
        for r in range(1, rep):
            imp = imp + p_t[:, r * tq:(r + 1) * tq]
        imp = jnp.where(forced | ~seen, -jnp.inf, imp)
        sel = (forced | _top_blocks(imp, SEL_TOPK - 3)) & seen
        bias = jnp.where(sel & ~own, 0.0, NEG_INF).T
        q_aug.append(jnp.concatenate([q, _tile_heads(bias, rep).astype(BF16)], axis=1))
        s_w = s_win[g]
        e_w = jnp.exp2(s_w - jnp.max(s_w, axis=1, keepdims=True))
        l_w = jnp.sum(e_w, axis=1, keepdims=True)
        e_w = e_w.astype(BF16)
        o_w = _dot(e_w[:, 0:tq], tiles[0])
        for c in range(1, ntile):
            o_w = o_w + _dot(e_w[:, c * tq:(c + 1) * tq], tiles[c])
        o_win.append(o_w / l_w)

    def sel_step(j, carry):
        sl = pl.ds(pl.multiple_of(j * tk, tk), tk)
        scores = [_dot_nt(q_aug[g], ka_ref[0, g, sl, :]) for g in range(NSA_GROUPS)]
        return tuple(_flash_update(*carry[g], scores[g], va_ref[0, g, sl, :]) for g in range(NSA_GROUPS))

    init = tuple((jnp.full((rows, 1), NEG_INF, F32), jnp.zeros((rows, LANES), F32)) for _ in range(NSA_GROUPS))
    carry = lax.fori_loop(0, start // tk + 1, sel_step, init)
    own_keys = pl.ds(pl.multiple_of(start, tq), tq)
    scores = [jnp.where(tri, _dot_nt(qs[g], ka_ref[0, g, own_keys, 0:LANES]), NEG_INF) for g in range(NSA_GROUPS)]
    carry = tuple(_flash_update(*carry[g], scores[g], va_ref[0, g, own_keys, :]) for g in range(NSA_GROUPS))

    heads = []
    for g in range(NSA_GROUPS):
        acc = carry[g][1]
        o_s = acc / acc[:, 0:1]
        for r in range(rep):
            hh = g * rep + r
            rs = slice(r * tq, (r + 1) * tq)
            heads.append(gates[:, 3 * hh:3 * hh + 1] * o_cmp[g][rs] + gates[:, 3 * hh + 1:3 * hh + 2] * o_s[rs]
                         + gates[:, 3 * hh + 2:3 * hh + 3] * o_win[g][rs])
    low = lax.broadcasted_iota(jnp.int32, (tq, LANES), 1) < NSA_DIM
    slabs = [jnp.where(low, pltpu.roll(heads[2 * k], NSA_DIM, 1), heads[2 * k + 1]) for k in range(NSA_HEADS // 2)]
    o_ref[0] = jnp.concatenate(slabs, axis=1).astype(o_ref.dtype)


def _nsa_prompt(q, ka, va, win, pool, gates, tq, tk):
    nb, nh, ns, _ = q.shape
    nblk = ns // CMP_BLOCK
    assert WINDOW % tq == 0 and tk % tq == 0
    kern = functools.partial(_nsa_prompt_kernel, tq=tq, tk=tk)
    whole = lambda a: pl.BlockSpec((1,) + a.shape[1:], lambda b, i: (b, 0, 0, 0))
    return pl.pallas_call(
        kern, grid=(nb, ns // tq),
        in_specs=[pl.BlockSpec((1, nh, tq, LANES), lambda b, i: (b, 0, i, 0)),
                  whole(ka), whole(va), whole(win),
                  pl.BlockSpec((1, NSA_GROUPS, nblk, LANES), lambda b, i: (b, 0, 0, 0)),
                  pl.BlockSpec((1, tq, LANES), lambda b, i: (b, i, 0))],
        out_specs=pl.BlockSpec((1, tq, nh * NSA_DIM), lambda b, i: (b, i, 0)),
        out_shape=jax.ShapeDtypeStruct((nb, ns, nh * NSA_DIM), BF16),
        compiler_params=_params(("arbitrary", "arbitrary")), name="nsa_prompt",
    )(q, ka, va, win, pool, gates)


def _head_matmul_kernel(a_ref, b_ref, o_ref):
    o_ref[0] = _dot(a_ref[0], b_ref[0]).astype(o_ref.dtype)


def _head_matmul(a, b, dtype):
    nh, m, k = a.shape
    n = b.shape[2]
    return pl.pallas_call(
        _head_matmul_kernel, grid=(nh,),
        in_specs=[pl.BlockSpec((1, m, k), lambda h: (h, 0, 0)), pl.BlockSpec((1, k, n), lambda h: (h, 0, 0))],
        out_specs=pl.BlockSpec((1, m, n), lambda h: (h, 0, 0)),
        out_shape=jax.ShapeDtypeStruct((nh, m, n), dtype),
        compiler_params=_params(("arbitrary",)), name="head_matmul",
    )(a, b)


def _new_token_mask(rows, t_new):
    tok = lax.broadcasted_iota(jnp.int32, (rows, LANES), 0) % t_new
    lane = lax.broadcasted_iota(jnp.int32, (rows, LANES), 1)
    return (lane < t_new) & (lane <= tok)


def _fetch_pages(pt_ref, cache_ref, buf, sem, pages):
    b = pl.program_id(0)
    slot = b % 2

    def copy(seq, p, s):
        return pltpu.make_async_copy(cache_ref.at[pt_ref[seq, p]], buf.at[s, p], sem.at[s])

    def start(seq, s):
        def body(p, c):
            copy(seq, p, s).start()
            return c
        lax.fori_loop(0, pages, body, 0)

    @pl.when(b == 0)
    def _():
        start(0, 0)

    @pl.when(b + 1 < pl.num_programs(0))
    def _():
        start(b + 1, 1 - slot)

    def wait(p, c):
        copy(b, p, slot).wait()
        return c
    lax.fori_loop(0, pages, wait, 0)
    return slot


def _page_scratch(pages, feat):
    return [pltpu.VMEM((2, pages, feat, PAGE_SIZE), F32), pltpu.SemaphoreType.DMA((2,))]


def _mla_sample_kernel(pt_ref, q_ref, new_ref, cache_ref, o_ref, kv_scr, new_scr, m_scr, l_scr, acc_scr,
                       page_buf, page_sem, *, pages, t_new):
    slot = _fetch_pages(pt_ref, cache_ref, page_buf, page_sem, pages)
    j = pl.program_id(1)
    width = kv_scr.shape[0]

    @pl.when((pl.program_id(0) == 0) & (j == 0))
    def _():
        kv_scr[...] = jnp.zeros(kv_scr.shape, BF16)

    @pl.when(j == 0)
    def _():
        m0, l0, a0 = _flash_init(q_ref.shape[1], width)
        m_scr[...], l_scr[...], acc_scr[...] = m0, l0, a0

    for p in range(pages):
        kv_scr[0:MLA_ROW, p * PAGE_SIZE:(p + 1) * PAGE_SIZE] = page_buf[slot, p].astype(BF16)
    q = q_ref[0]
    kv = kv_scr[...]
    carry = _online_update((m_scr[...], l_scr[...], acc_scr[...]), _dot(q, kv), lambda p: _dot_nt(p, kv))
    m_scr[...], l_scr[...], acc_scr[...] = carry

    @pl.when(j == pl.num_programs(1) - 1)
    def _():
        new_scr[...] = jnp.zeros(new_scr.shape, BF16)
        new_scr[0:t_new, 0:MLA_ROW] = new_ref[0].astype(BF16)
        kvn = new_scr[...]
        s = jnp.where(_new_token_mask(q.shape[0], t_new), _dot_nt(q, kvn), NEG_INF)
        _, l, acc = _online_update(carry, s, lambda p: _dot(p, kvn))
        o_ref[0] = acc / l


def _mla_sample(page_table, q, new_rows, cache_t, pages):
    nb, rows, width = q.shape
    t_new = new_rows.shape[1]
    n_pages = page_table.shape[1]
    kern = functools.partial(_mla_sample_kernel, pages=pages, t_new=t_new)
    grid_spec = pltpu.PrefetchScalarGridSpec(
        num_scalar_prefetch=1, grid=(nb, n_pages // pages),
        in_specs=[pl.BlockSpec((1, rows, width), lambda b, j, pt: (b, 0, 0)),
                  pl.BlockSpec((1, t_new, MLA_ROW), lambda b, j, pt: (b, 0, 0)), pl.BlockSpec(memory_space=pl.ANY)],
        out_specs=pl.BlockSpec((1, rows, width), lambda b, j, pt: (b, 0, 0)),
        scratch_shapes=[pltpu.VMEM((width, pages * PAGE_SIZE), BF16), pltpu.VMEM((LANES, width), BF16),
                        pltpu.VMEM((rows, 1), F32), pltpu.VMEM((rows, 1), F32), pltpu.VMEM((rows, width), F32)]
        + _page_scratch(pages, MLA_ROW))
    return pl.pallas_call(
        kern, grid_spec=grid_spec, out_shape=jax.ShapeDtypeStruct((nb, rows, width), F32),
        compiler_params=_params(("arbitrary", "arbitrary")), name="mla_sample",
    )(page_table, q, new_rows, cache_t)


def _cmp_pool_kernel(pt_ref, w_ref, e_ref, cache_ref, o_ref, hi_scr, lo_scr, page_buf, page_sem, *, pages):
    slot = _fetch_pages(pt_ref, cache_ref, page_buf, page_sem, pages)
    w = w_ref[...]
    for p in range(pages):
        x = page_buf[slot, p] * w
        hi = x.astype(BF16)
        sl = slice(p * PAGE_SIZE, (p + 1) * PAGE_SIZE)
        hi_scr[:, sl] = hi
        lo_scr[:, sl] = (x - hi.astype(F32)).astype(BF16)
    e = e_ref[...]
    o_ref[0] = _dot_nt(e, hi_scr[...]) + _dot_nt(e, lo_scr[...])


def _cmp_pool(page_table, w_pool_t, cache_t, pages):
    nb, n_pages = page_table.shape
    feat = cache_t.shape[1]
    per_step = pages * PAGE_SIZE // CMP_BLOCK
    expand = _block_expand(per_step, pages * PAGE_SIZE)
    kern = functools.partial(_cmp_pool_kernel, pages=pages)
    grid_spec = pltpu.PrefetchScalarGridSpec(
        num_scalar_prefetch=1, grid=(nb, n_pages // pages),
        in_specs=[pl.BlockSpec(w_pool_t.shape, lambda b, j, pt: (0, 0)),
                  pl.BlockSpec(expand.shape, lambda b, j, pt: (0, 0)), pl.BlockSpec(memory_space=pl.ANY)],
        out_specs=pl.BlockSpec((1, per_step, feat), lambda b, j, pt: (b, j, 0)),
        scratch_shapes=[pltpu.VMEM((feat, pages * PAGE_SIZE), BF16), pltpu.VMEM((feat, pages * PAGE_SIZE), BF16)]
        + _page_scratch(pages, feat))
    return pl.pallas_call(
        kern, grid_spec=grid_spec,
        out_shape=jax.ShapeDtypeStruct((nb, n_pages * PAGE_SIZE // CMP_BLOCK, feat), F32),
        compiler_params=_params(("arbitrary", "arbitrary")), name="cmp_pool",
    )(page_table, w_pool_t, expand, cache_t)


def _pack_group(rows, g):
    ko, vo = g * NSA_DIM, NSA_GROUPS * NSA_DIM + g * NSA_DIM
    return jnp.concatenate([rows[:, ko:ko + NSA_DIM], rows[:, vo:vo + NSA_DIM]], axis=1).astype(BF16)


def _pack_group_t(feats, g):
    ko, vo = g * NSA_DIM, NSA_GROUPS * NSA_DIM + g * NSA_DIM
    return jnp.concatenate([feats[ko:ko + NSA_DIM], feats[vo:vo + NSA_DIM]], axis=0).astype(BF16)


def _nsa_sample_kernel(pt_ref, q_ref, pool_ref, gate_ref, win_ref, wnew_ref, snew_ref, expand_ref, cache_ref,
                       o_ref, kv_scr, new_scr, q_scr, sel_scr, oc_scr, ow_scr, m_scr, l_scr, acc_scr,
                       page_buf, page_sem, *, pages, t_new):
    slot = _fetch_pages(pt_ref, cache_ref, page_buf, page_sem, pages)
    j = pl.program_id(1)
    rep = NSA_REP
    rows = rep * t_new
    nblk = pool_ref.shape[1]
    tk = pages * PAGE_SIZE
    new_ok = _new_token_mask(rows, t_new)

    def new_tile(ref, g):
        new_scr[...] = jnp.zeros(new_scr.shape, BF16)
        new_scr[0:t_new, :] = _pack_group(ref[0], g)
        return new_scr[...]

    @pl.when(j == 0)
    def _():
        pool = pool_ref[0]
        win_t = win_ref[0]
        wb = win_t.shape[1]
        tok = lax.broadcasted_iota(jnp.int32, (rows, 1), 0) % t_new
        lane = lax.broadcasted_iota(jnp.int32, (nblk, LANES), 1)
        blk = lax.broadcasted_iota(jnp.int32, (nblk, LANES), 0)
        for g in range(NSA_GROUPS):
            q = q_ref[0, g * rows:(g + 1) * rows]
            pg = _pack_group(pool, g)
            oc_scr[g] = _dot(_softmax(_dot_nt(q, pg), axis=1).astype(BF16), pg)
            q_scr[...] = jnp.zeros(q_scr.shape, BF16)
            q_scr[0:rows, :] = q
            p_t = _softmax(_dot_nt(pg, q_scr[...]), axis=0)
            tot = p_t
            for r in range(1, rep):
                tot = tot + pltpu.roll(p_t, LANES - r * t_new, 1)
            tot = jnp.where(lane < t_new, tot, 0.0)
            imp = tot
            for r in range(1, rep):
                imp = imp + pltpu.roll(tot, r * t_new, 1)
            forced = (blk == 0) | (blk == nblk - 1)
            sel = forced | _top_blocks(jnp.where(forced, -jnp.inf, imp), SEL_TOPK - 3)
            sel_scr[g] = jnp.where(sel, 1.0, 0.0).T[0:rows]
            wg = _pack_group_t(win_t, g)
            i = lax.broadcasted_iota(jnp.int32, (1, wb), 1)
            ng = new_tile(wnew_ref, g)
            s_w = jnp.concatenate([jnp.where(i > tok + (wb - WINDOW), _dot(q, wg), NEG_INF),
                                   jnp.where(new_ok, _dot_nt(q, ng), NEG_INF)], axis=1)
            p_w = _softmax(s_w, axis=1).astype(BF16)
            ow_scr[g] = _dot_nt(p_w[:, :wb], wg) + _dot(p_w[:, wb:], ng)
            m0, l0, a0 = _flash_init(rows, LANES)
            m_scr[g], l_scr[g], acc_scr[g] = m0, l0, a0

    for p in range(pages):
        page = page_buf[slot, p]
        for g in range(NSA_GROUPS):
            kv_scr[g, :, p * PAGE_SIZE:(p + 1) * PAGE_SIZE] = _pack_group_t(page, g)
    sl = pl.ds(pl.multiple_of(j * tk, tk), tk)
    for g in range(NSA_GROUPS):
        q = q_ref[0, g * rows:(g + 1) * rows]
        kv = kv_scr[g]
        hit = _dot(sel_scr[g].astype(BF16), expand_ref[:, sl])
        s = _dot(q, kv) + (hit - 1.0) * (-NEG_INF)
        carry = _online_update((m_scr[g], l_scr[g], acc_scr[g]), s, lambda p, kv=kv: _dot_nt(p, kv))
        m_scr[g], l_scr[g], acc_scr[g] = carry

    @pl.when(j == pl.num_programs(1) - 1)
    def _():
        gates = gate_ref[0]
        for g in range(NSA_GROUPS):
            q = q_ref[0, g * rows:(g + 1) * rows]
            ng = new_tile(snew_ref, g)
            s = jnp.where(new_ok, _dot_nt(q, ng), NEG_INF)
            _, l, acc = _online_update((m_scr[g], l_scr[g], acc_scr[g]), s, lambda p, ng=ng: _dot(p, ng))
            gg = gates[g * rows:(g + 1) * rows]
            o_ref[0, g * rows:(g + 1) * rows, :] = (gg[:, 0:1] * oc_scr[g] + gg[:, 1:2] * (acc / l)
                                                    + gg[:, 2:3] * ow_scr[g])


def _nsa_sample(page_table, q, pool, gates, win_t, win_new, slc_new, expand, cache_t, pages):
    nb, rows2, _ = q.shape
    t_new = slc_new.shape[1]
    n_pages = page_table.shape[1]
    feat = cache_t.shape[1]
    rows = rows2 // NSA_GROUPS
    per_b = lambda shape: pl.BlockSpec((1,) + shape, lambda b, j, pt: (b, 0, 0))
    kern = functools.partial(_nsa_sample_kernel, pages=pages, t_new=t_new)
    grp = lambda *shape: pltpu.VMEM((NSA_GROUPS,) + shape, F32)
    grid_spec = pltpu.PrefetchScalarGridSpec(
        num_scalar_prefetch=1, grid=(nb, n_pages // pages),
        in_specs=[per_b((rows2, LANES)), per_b(pool.shape[1:]), per_b((rows2, LANES)), per_b(win_t.shape[1:]),
                  per_b((t_new, feat)), per_b((t_new, feat)),
                  pl.BlockSpec(expand.shape, lambda b, j, pt: (0, 0)), pl.BlockSpec(memory_space=pl.ANY)],
        out_specs=per_b((rows2, LANES)),
        scratch_shapes=[pltpu.VMEM((NSA_GROUPS, LANES, pages * PAGE_SIZE), BF16), pltpu.VMEM((LANES, LANES), BF16),
                        pltpu.VMEM((LANES, LANES), BF16), grp(rows, expand.shape[0]),
                        grp(rows, LANES), grp(rows, LANES), grp(rows, 1), grp(rows, 1), grp(rows, LANES)]
        + _page_scratch(pages, feat))
    return pl.pallas_call(
        kern, grid_spec=grid_spec, out_shape=jax.ShapeDtypeStruct((nb, rows2, LANES), F32),
        compiler_params=_params(("arbitrary", "arbitrary")), name="nsa_sample",
    )(page_table, q, pool, gates, win_t, win_new, slc_new, expand, cache_t)


def _tail_kernel(x_ref, ya_ref, yb_ref, ga_ref, gb_ref, p_ref, wba_ref, wbn_ref, wo_ref, lnf_ref,
                 wg_ref, wu_ref, wd_ref, lnp_ref, wpg_ref, wpp_ref, lnl_ref, o_ref, hid_scr, *, chunk):
    merged = ga_ref[...] * _dot(ya_ref[...], wba_ref[...]) + gb_ref[...] * _dot(yb_ref[...], wbn_ref[...])
    x = x_ref[...] + _dot(merged.astype(BF16), wo_ref[...])
    h = _rms(x, lnf_ref[...]).astype(BF16)

    def ffn(c, carry):
        sl = pl.ds(pl.multiple_of(c * chunk, chunk), chunk)
        hid_scr[:, sl] = (jax.nn.silu(_dot(h, wg_ref[:, sl])) * _dot(h, wu_ref[:, sl])).astype(BF16)
        return carry

    lax.fori_loop(0, wg_ref.shape[1] // chunk, ffn, 0)
    x = x + _dot(hid_scr[...], wd_ref[...])
    gate = jax.nn.sigmoid(_dot(_rms(x, lnp_ref[...]).astype(BF16), wpg_ref[...]))
    x = x + gate * _dot(p_ref[...].astype(BF16), wpp_ref[...])
    o_ref[...] = _rms(x, lnl_ref[...])


def _tail(x, ya, yb, ga, gb, p, weights, tm):
    n, d_model = x.shape
    row = lambda a: pl.BlockSpec((tm, a.shape[1]), lambda i: (i, 0))
    acts = (x, ya, yb, ga, gb, p)
    kern = functools.partial(_tail_kernel, chunk=2 * LANES)
    return pl.pallas_call(
        kern, grid=(n // tm,),
        in_specs=[row(a) for a in acts] + [_resident(w.shape) for w in weights],
        out_specs=pl.BlockSpec((tm, d_model), lambda i: (i, 0)),
        out_shape=jax.ShapeDtypeStruct((n, d_model), F32),
        scratch_shapes=[pltpu.VMEM((tm, weights[4].shape[1]), BF16)],
        compiler_params=_params(("arbitrary",)), name="tail",
    )(*acts, *weights)


def _rope_tables(pos):
    pos = pos.astype(F32)[:, None]
    n = pos.shape[0]

    def cs(rot):
        half = rot // 2
        inv = jnp.float32(ROPE_THETA) ** (-jnp.arange(half, dtype=F32) * 2.0 / rot)
        ang = pos * inv[None, :]
        return jnp.cos(ang), jnp.sin(ang)

    one = lambda w: jnp.ones((n, w), F32)
    zero = lambda w: jnp.zeros((n, w), F32)
    c, s = cs(MLA_ROPE)
    pad = LANES - MLA_NOPE - MLA_ROPE
    cm = jnp.concatenate([one(MLA_NOPE), c, c, one(pad)], axis=1)
    sm = jnp.concatenate([zero(MLA_NOPE), -s, s, zero(pad)], axis=1)
    c, s = cs(NSA_ROT)
    rest = NSA_DIM - NSA_ROT
    cn = jnp.concatenate([c, c, one(rest)] * (LANES // NSA_DIM), axis=1)
    sn = jnp.concatenate([-s, s, zero(rest)] * (LANES // NSA_DIM), axis=1)
    return cm, sm, cn, sn


def _layout_w_in(w_in):
    d_model = w_in.shape[0]
    sizes = (MLA_HEADS * (MLA_NOPE + MLA_ROPE), MLA_KV_LORA, MLA_ROPE, NSA_HEADS * NSA_DIM,
             SEG_KV // 3, SEG_KV // 3, SEG_KV // 3, 3 * NSA_HEADS, 2 * d_model)
    offs = np.concatenate([[0], np.cumsum(sizes)])
    qa, ca, ra, qb, zc, zs, zw, gn, gm = [w_in[:, int(offs[i]):int(offs[i + 1])] for i in range(len(sizes))]
    qa = qa.reshape(d_model, MLA_HEADS, MLA_NOPE + MLA_ROPE)
    qa = jnp.pad(qa, ((0, 0), (0, 0), (0, LANES - MLA_NOPE - MLA_ROPE))).reshape(d_model, SEG_QA)
    ra = jnp.pad(ra, ((0, 0), (MLA_NOPE, LANES - MLA_NOPE - MLA_ROPE)))
    gn = jnp.pad(gn, ((0, 0), (0, SEG_GN - gn.shape[1])))
    return jnp.concatenate([qa, ca, ra, qb, zc, zs, zw, gn, gm], axis=1).astype(BF16)


def _layout_w_up(w_uk, w_uv):
    wk = jnp.pad(w_uk, ((0, 0), (0, 0), (0, LANES - MLA_NOPE))).reshape(MLA_KV_LORA, SEG_QA)
    return jnp.concatenate([wk, w_uv.reshape(MLA_KV_LORA, MLA_HEADS * MLA_V)], axis=1).astype(BF16)


def _layout_absorb(w_uk, w_uv, width):
    to_lat = jnp.zeros((MLA_HEADS, LANES, width), F32)
    to_lat = to_lat.at[:, 0:MLA_NOPE, 0:MLA_KV_LORA].set(w_uk.transpose(1, 2, 0))
    to_lat = to_lat.at[:, MLA_NOPE:MLA_NOPE + MLA_ROPE, MLA_KV_LORA:MLA_ROW].set(jnp.eye(MLA_ROPE, dtype=F32))
    to_val = jnp.zeros((MLA_HEADS, width, MLA_V), F32).at[:, 0:MLA_KV_LORA, :].set(w_uv.transpose(1, 0, 2))
    return to_lat.astype(BF16), to_val.astype(BF16)


def _block_expand(nblk, nkeys):
    return (jnp.arange(nkeys)[None, :] // CMP_BLOCK == jnp.arange(nblk)[:, None]).astype(BF16)


def kernel(x_prompt, x_sample, cache_mla, cache_nsa_cmp, cache_nsa_slc, state_nsa_win, page_table, p_prompt, p_sample, ln_attn, w_in, mla_kv_norm, mla_w_uk, mla_w_uv, nsa_w_cmp_k, nsa_w_cmp_v, w_branch_mla, w_branch_nsa, w_out, ln_ffn, w_ffn_gate, w_ffn_up, w_ffn_down, ln_ple, w_ple_gate, w_ple_proj, ln_final):
    nb, ns, d_model = x_prompt.shape
    db, t_new, _ = x_sample.shape
    depth = w_in.shape[0]
    n_pages = page_table.shape[1]
    past = n_pages * PAGE_SIZE
    wb = state_nsa_win.shape[2]
    width = 2 * NSA_GROUPS * NSA_DIM
    g_, d_ = NSA_GROUPS, NSA_DIM
    assert depth == 1 and t_new <= CMP_BLOCK and past % CMP_BLOCK == 0 and wb == WINDOW and past >= wb
    assert ns % 512 == 0 and (db * t_new) % 512 == 0 and NSA_REP * t_new <= LANES
    tm = 512
    pages = n_pages
    assert pages <= 64
    lat_w = 3 * LANES
    i = 0

    w_all = _layout_w_in(w_in[i])
    w_up = _layout_w_up(mla_w_uk[i], mla_w_uv[i])
    to_lat, to_val = _layout_absorb(mla_w_uk[i], mla_w_uv[i], lat_w)
    w_pool = jnp.concatenate([nsa_w_cmp_k[i]] * g_ + [nsa_w_cmp_v[i]] * g_, axis=1)
    w_pool_t = jnp.tile(w_pool.T, (1, PAGE_SIZE // CMP_BLOCK))
    ln_a = ln_attn[i][None, :]
    kvn = mla_kv_norm[i][None, :]
    tail_w = (w_branch_mla[i].astype(BF16), w_branch_nsa[i].astype(BF16), w_out[i].astype(BF16),
              ln_ffn[i][None, :], w_ffn_gate[i].astype(BF16), w_ffn_up[i].astype(BF16),
              w_ffn_down[i].astype(BF16), ln_ple[i][None, :], w_ple_gate[i].astype(BF16),
              w_ple_proj[i].astype(BF16), ln_final[None, :])

    xp = x_prompt.reshape(nb * ns, d_model)
    nblk = ns // CMP_BLOCK
    (qm, km, vm, mla_p, qn, kvc_p, kvs_p, kvw_p, ka, va, win, pool, gn, ga, gb) = _proj(
        xp, _rope_tables(jnp.arange(ns)), ln_a, w_all, w_up, kvn, w_pool, nb, ns, tm, nblk)
    token_major = lambda a: a.transpose(0, 2, 1).reshape(-1, a.shape[1])
    mla_p, kvc_p, kvs_p, kvw_p = (token_major(a) for a in (mla_p, kvc_p, kvs_p, kvw_p))
    ya = _mla_flash(qm, km, vm, 512)
    pool5 = pool.reshape(nb, nblk, 2, g_, d_)
    pool_g = jnp.concatenate([pool5[:, :, 0].transpose(0, 2, 1, 3), pool5[:, :, 1].transpose(0, 2, 1, 3)], axis=-1)
    yb = _nsa_prompt(qn, ka, va, win, pool_g, gn.reshape(nb, ns, LANES), 128, 512)
    y_prompt = _tail(xp, ya.reshape(nb * ns, -1), yb.reshape(nb * ns, -1), ga, gb,
                     p_prompt[i].reshape(nb * ns, -1), tail_w, tm).reshape(nb, ns, d_model)

    n_s = db * t_new
    xs = x_sample.reshape(n_s, d_model)
    pos_s = jnp.tile(past + jnp.arange(t_new), db)
    (qm, _, _, mla_s, qn, kvc_s, kvs_s, kvw_s, _, _, _, _, gn, ga, gb) = _proj(
        xs, _rope_tables(pos_s), ln_a, w_all, w_up, kvn, w_pool, 1, n_s, tm, LANES)
    mla_s, kvc_s, kvs_s, kvw_s = (token_major(a) for a in (mla_s, kvc_s, kvs_s, kvw_s))
    by_batch = lambda a: a.reshape(a.shape[0], db, t_new, a.shape[-1]).transpose(1, 0, 2, 3).reshape(
        db, a.shape[0] * t_new, a.shape[-1])
    q_lat = by_batch(_head_matmul(qm[0], to_lat, BF16))
    mla_t = cache_mla[i].transpose(0, 2, 1)
    o_lat = _mla_sample(page_table, q_lat, mla_s.reshape(db, t_new, MLA_ROW), mla_t, pages)
    o_lat = o_lat.reshape(db, MLA_HEADS, t_new, lat_w).transpose(1, 0, 2, 3).reshape(MLA_HEADS, n_s, lat_w)
    ya = _head_matmul(o_lat.astype(BF16), to_val, BF16).transpose(1, 0, 2).reshape(n_s, MLA_HEADS * MLA_V)

    cmp_t = cache_nsa_cmp[i].reshape(-1, PAGE_SIZE, width).transpose(0, 2, 1)
    slc_t = cache_nsa_slc[i].reshape(-1, PAGE_SIZE, width).transpose(0, 2, 1)
    pool_s = _cmp_pool(page_table, w_pool_t, cmp_t, pages)
    gate_s = gn[:, :3 * NSA_HEADS].reshape(db, t_new, NSA_HEADS, 3).transpose(0, 2, 1, 3).reshape(
        db, NSA_HEADS * t_new, 3)
    gate_s = jnp.pad(gate_s, ((0, 0), (0, 0), (0, LANES - 3)))
    win_state = state_nsa_win[i].reshape(db, wb, width)
    o_nsa = _nsa_sample(page_table, by_batch(qn[0]), pool_s, gate_s, win_state.transpose(0, 2, 1),
                        kvw_s.reshape(db, t_new, width), kvs_s.reshape(db, t_new, width),
                        _block_expand(past // CMP_BLOCK, past), slc_t, pages)
    yb = o_nsa[:, :, NSA_DIM:].reshape(db, NSA_HEADS, t_new, d_).transpose(0, 2, 1, 3).reshape(n_s, -1)
    y_sample = _tail(xs, ya, yb.astype(BF16), ga, gb, p_sample[i].reshape(n_s, -1), tail_w, tm).reshape(
        db, t_new, d_model)

    kv5 = lambda a, b, s: a.reshape(1, b, s, 2, g_, d_)
    win_all = jnp.concatenate([win_state, kvw_s.reshape(db, t_new, width)], axis=1)
    wkeep = min(WINDOW, ns)
    return (y_prompt, y_sample,
            mla_p.reshape(1, nb, ns, MLA_ROW), mla_s.reshape(1, db, t_new, MLA_ROW),
            kv5(kvc_p, nb, ns), kv5(kvc_s, db, t_new), kv5(kvs_p, nb, ns), kv5(kvs_s, db, t_new),
            kv5(kvw_p, nb, ns)[:, :, ns - wkeep:],
            kv5(win_all[:, -min(WINDOW, past + t_new):], db, min(WINDOW, past + t_new)))
```

```python
import functools
import math

import jax
import jax.numpy as jnp
import numpy as np
from jax import lax
from jax.experimental import pallas as pl
from jax.experimental.pallas import tpu as pltpu

F32 = jnp.float32
BF16 = jnp.bfloat16

NORM_EPS = 1e-6
ROPE_THETA = 500000.0
NEG_INF = -1e30
LOG2E = math.log2(math.e)
PAGE_SIZE = 128
MLA_HEADS = 8
MLA_NOPE = 64
MLA_ROPE = 32
MLA_V = 64
MLA_KV_LORA = 256
MLA_ROW = MLA_KV_LORA + MLA_ROPE
NSA_HEADS = 8
NSA_GROUPS = 2
NSA_REP = NSA_HEADS // NSA_GROUPS
NSA_DIM = 64
NSA_ROT = NSA_DIM // 4
CMP_BLOCK = 64
SEL_TOPK = 16
WINDOW = 512
LANES = 128
VMEM_LIMIT = 60 * 1024 * 1024

SEG_QA = MLA_HEADS * LANES
SEG_C = MLA_KV_LORA
SEG_R = LANES
SEG_QB = NSA_HEADS * NSA_DIM
SEG_KV = 3 * 2 * NSA_GROUPS * NSA_DIM
SEG_GN = LANES
OFF_C = SEG_QA
OFF_R = OFF_C + SEG_C
OFF_QB = OFF_R + SEG_R
OFF_KV = OFF_QB + SEG_QB
OFF_GN = OFF_KV + SEG_KV
OFF_GM = OFF_GN + SEG_GN


def _params(sem):
    return pltpu.CompilerParams(dimension_semantics=sem, vmem_limit_bytes=VMEM_LIMIT)


def _resident(shape):
    nd = len(shape)
    return pl.BlockSpec(shape, lambda *_: (0,) * nd, pipeline_mode=pl.Buffered(1))


def _dot(a, b):
    return jnp.dot(a, b, preferred_element_type=F32)


def _dot_nt(a, b):
    return lax.dot_general(a, b, (((1,), (1,)), ((), ())), preferred_element_type=F32)


def _rms(x, g):
    return x * lax.rsqrt(jnp.mean(x * x, axis=-1, keepdims=True) + NORM_EPS) * g


def _value_slot(v, rows):
    lane = lax.broadcasted_iota(jnp.int32, (rows, LANES - v.shape[1]), 1)
    return jnp.concatenate([v, jnp.where(lane == 0, 1.0, 0.0)], axis=1).astype(BF16)


def _proj_kernel(x_ref, ln_ref, w_ref, wup_ref, kvn_ref, cm_ref, sm_ref, cn_ref, sn_ref, wpool_ref,
                 qm_ref, km_ref, vm_ref, mla_ref, qn_ref, kvc_ref, kvs_ref, kvw_ref,
                 ka_ref, va_ref, win_ref, pool_ref, gn_ref, ga_ref, gb_ref, *, nt):
    tm = x_ref.shape[0]
    d_model = x_ref.shape[1]
    nblk = ka_ref.shape[3] - LANES
    h = _rms(x_ref[...], ln_ref[...]).astype(BF16)
    lane = lax.broadcasted_iota(jnp.int32, (tm, LANES), 1)
    cm, sm, cn, sn = cm_ref[...], sm_ref[...], cn_ref[...], sn_ref[...]

    def rope_m(z):
        half = MLA_ROPE // 2
        sw = jnp.where(lane < MLA_NOPE + half, pltpu.roll(z, LANES - half, 1), pltpu.roll(z, half, 1))
        return z * cm + sw * sm

    def rope_n(z):
        half = NSA_ROT // 2
        sw = jnp.where((lane & (NSA_DIM - 1)) < half, pltpu.roll(z, LANES - half, 1), pltpu.roll(z, half, 1))
        return z * cn + sw * sn

    zq = _dot(h, w_ref[:, 0:SEG_QA])
    scale_m = (MLA_NOPE + MLA_ROPE) ** -0.5 * LOG2E
    for hh in range(MLA_HEADS):
        qm_ref[0, hh] = (rope_m(zq[:, hh * LANES:(hh + 1) * LANES]) * scale_m).astype(BF16)

    zc = _dot(h, w_ref[:, OFF_C:OFF_R])
    c = _rms(zc, kvn_ref[...])
    kpe = rope_m(_dot(h, w_ref[:, OFF_R:OFF_QB]))
    mla_ref[0, 0:MLA_KV_LORA, :] = c.T
    mla_ref[0, MLA_KV_LORA:MLA_ROW, :] = kpe.T[MLA_NOPE:MLA_NOPE + MLA_ROPE]
    up = _dot(c.astype(BF16), wup_ref[...])
    for hh in range(MLA_HEADS):
        km_ref[0, hh] = (up[:, hh * LANES:(hh + 1) * LANES] + kpe).astype(BF16)
        vo = SEG_QA + hh * MLA_V
        vm_ref[0, hh] = _value_slot(up[:, vo:vo + MLA_V], tm)

    zqb = _dot(h, w_ref[:, OFF_QB:OFF_KV])
    zero_half = jnp.zeros((tm, NSA_DIM), F32)
    scale_n = NSA_DIM ** -0.5 * LOG2E
    for pair in range(NSA_HEADS // 2):
        zz = rope_n(zqb[:, pair * LANES:(pair + 1) * LANES]) * scale_n
        qn_ref[0, 2 * pair] = jnp.concatenate([zz[:, :NSA_DIM], zero_half], axis=1).astype(BF16)
        qn_ref[0, 2 * pair + 1] = jnp.concatenate([zz[:, NSA_DIM:], zero_half], axis=1).astype(BF16)

    zkv = _dot(h, w_ref[:, OFF_KV:OFF_GN])
    width = 2 * NSA_GROUPS * NSA_DIM

    def kv_rows(i):
        z = zkv[:, i * width:(i + 1) * width]
        return jnp.concatenate([rope_n(z[:, :LANES]), z[:, LANES:]], axis=1)

    rows_c, rows_s, rows_w = kv_rows(0), kv_rows(1), kv_rows(2)
    kvc_ref[0] = rows_c.T
    kvs_ref[0] = rows_s.T
    kvw_ref[0] = rows_w.T
    pos = (pl.program_id(0) % nt) * tm + lax.broadcasted_iota(jnp.int32, (tm, nblk), 0)
    onehot = jnp.where(lax.broadcasted_iota(jnp.int32, (tm, nblk), 1) == pos // CMP_BLOCK, 1.0, 0.0)
    for g in range(NSA_GROUPS):
        ko, vo = g * NSA_DIM, LANES + g * NSA_DIM
        ka_ref[0, g] = jnp.concatenate([rows_s[:, ko:ko + NSA_DIM], zero_half, onehot], axis=1).astype(BF16)
        unit = jnp.where(lax.broadcasted_iota(jnp.int32, (tm, NSA_DIM), 1) == 0, 1.0, 0.0)
        va_ref[0, g] = jnp.concatenate([unit, rows_s[:, vo:vo + NSA_DIM]], axis=1).astype(BF16)
        win_ref[0, g] = jnp.concatenate([rows_w[:, ko:ko + NSA_DIM], rows_w[:, vo:vo + NSA_DIM]],
                                        axis=1).astype(BF16)
    pool_ref[...] = jnp.sum(rows_c.reshape(tm // CMP_BLOCK, CMP_BLOCK, width) * wpool_ref[...][None], axis=1)

    gn_ref[...] = jax.nn.sigmoid(_dot(h, w_ref[:, OFF_GN:OFF_GM]))
    zm = _dot(h, w_ref[:, OFF_GM:OFF_GM + 2 * d_model])
    ga_ref[...] = jax.nn.sigmoid(zm[:, :d_model])
    gb_ref[...] = jax.nn.sigmoid(zm[:, d_model:])


def _proj(x, tabs, ln, w_all, w_up, kv_norm, w_pool, nb, ns, tm, nblk):
    n, d_model = x.shape
    nt = ns // tm
    width = 2 * NSA_GROUPS * NSA_DIM
    row = lambda w: pl.BlockSpec((tm, w), lambda i: (i, 0))
    head = lambda hn, w: pl.BlockSpec((1, hn, tm, w), lambda i: (i // nt, 0, i % nt, 0))
    tab = pl.BlockSpec((tm, LANES), lambda i: (i % nt, 0))
    feat = lambda f: pl.BlockSpec((1, f, tm), lambda i: (i // nt, 0, i % nt))
    slots = (
        (head(MLA_HEADS, LANES), (nb, MLA_HEADS, ns, LANES), BF16),
        (head(MLA_HEADS, LANES), (nb, MLA_HEADS, ns, LANES), BF16),
        (head(MLA_HEADS, LANES), (nb, MLA_HEADS, ns, LANES), BF16),
        (feat(MLA_ROW), (nb, MLA_ROW, ns), F32),
        (head(NSA_HEADS, LANES), (nb, NSA_HEADS, ns, LANES), BF16),
        (feat(width), (nb, width, ns), F32),
        (feat(width), (nb, width, ns), F32),
        (feat(width), (nb, width, ns), F32),
        (head(NSA_GROUPS, LANES + nblk), (nb, NSA_GROUPS, ns, LANES + nblk), BF16),
        (head(NSA_GROUPS, LANES), (nb, NSA_GROUPS, ns, LANES), BF16),
        (head(NSA_GROUPS, LANES), (nb, NSA_GROUPS, ns, LANES), BF16),
        (pl.BlockSpec((tm // CMP_BLOCK, width), lambda i: (i, 0)), (n // CMP_BLOCK, width), F32),
        (row(LANES), (n, LANES), F32),
        (row(d_model), (n, d_model), F32),
        (row(d_model), (n, d_model), F32),
    )
    in_specs = [row(d_model), _resident(ln.shape), _resident(w_all.shape), _resident(w_up.shape),
                _resident(kv_norm.shape), tab, tab, tab, tab, _resident(w_pool.shape)]
    return pl.pallas_call(
        functools.partial(_proj_kernel, nt=nt), grid=(n // tm,), in_specs=in_specs,
        out_specs=tuple(s[0] for s in slots),
        out_shape=tuple(jax.ShapeDtypeStruct(s[1], s[2]) for s in slots),
        compiler_params=_params(("arbitrary",)), name="proj",
    )(x, ln, w_all, w_up, kv_norm, *tabs, w_pool)


def _flash_update(m, acc, s, v):
    m_new = jnp.maximum(m, jnp.max(s, axis=-1, keepdims=True))
    alpha = jnp.exp2(m - m_new)
    p = jnp.exp2(s - m_new).astype(BF16)
    return m_new, alpha * acc + _dot(p, v)


def _flash_finish(acc, dv):
    return acc[:, :dv] / acc[:, dv:dv + 1]


def _online_update(carry, s, pv):
    m, l, acc = carry
    m_new = jnp.maximum(m, jnp.max(s, axis=-1, keepdims=True))
    alpha = jnp.exp2(m - m_new)
    p = jnp.exp2(s - m_new)
    l = alpha * l + jnp.sum(p, axis=-1, keepdims=True)
    acc = alpha * acc + pv(p.astype(BF16))
    return m_new, l, acc


def _flash_init(rows, dv):
    return (jnp.full((rows, 1), NEG_INF, F32), jnp.zeros((rows, 1), F32), jnp.zeros((rows, dv), F32))


def _mla_flash_kernel(q_ref, k_ref, v_ref, o_ref, *, t):
    qi = pl.program_id(2)
    heads = q_ref.shape[1]
    qs = [q_ref[0, hh] for hh in range(heads)]
    causal = lax.broadcasted_iota(jnp.int32, (t, t), 1) <= lax.broadcasted_iota(jnp.int32, (t, t), 0)

    def step(j, carry, mask):
        sl = pl.ds(pl.multiple_of(j * t, t), t)
        scores = [_dot_nt(qs[hh], k_ref[0, hh, sl, :]) for hh in range(heads)]
        if mask is not None:
            scores = [jnp.where(mask, s, NEG_INF) for s in scores]
        return tuple(_flash_update(*carry[hh], scores[hh], v_ref[0, hh, sl, :]) for hh in range(heads))

    init = tuple((jnp.full((t, 1), NEG_INF, F32), jnp.zeros((t, LANES), F32)) for _ in range(heads))
    carry = lax.fori_loop(0, qi, lambda j, c: step(j, c, None), init)
    carry = step(qi, carry, causal)
    o_ref[0] = jnp.concatenate([_flash_finish(acc, MLA_V) for _, acc in carry], axis=1).astype(o_ref.dtype)


def _mla_flash(q, k, v, t):
    nb, nh, ns, _ = q.shape
    hp = 8
    kern = functools.partial(_mla_flash_kernel, t=t)
    whole = pl.BlockSpec((1, hp, ns, LANES), lambda b, h, i: (b, h, 0, 0), pipeline_mode=pl.Buffered(1))
    return pl.pallas_call(
        kern, grid=(nb, nh // hp, ns // t),
        in_specs=[pl.BlockSpec((1, hp, t, LANES), lambda b, h, i: (b, h, i, 0)), whole, whole],
        out_specs=pl.BlockSpec((1, t, hp * MLA_V), lambda b, h, i: (b, i, h)),
        out_shape=jax.ShapeDtypeStruct((nb, ns, nh * MLA_V), BF16),
        compiler_params=_params(("arbitrary", "arbitrary", "arbitrary")), name="mla_flash",
    )(q, k, v)


def _masked_softmax(s, mask, axis):
    s = jnp.where(mask, s, NEG_INF)
    m = jnp.max(s, axis=axis, keepdims=True)
    e = jnp.where(mask, jnp.exp2(s - m), 0.0)
    return e / jnp.maximum(jnp.sum(e, axis=axis, keepdims=True), 1e-30)


def _softmax(s, axis):
    e = jnp.exp2(s - jnp.max(s, axis=axis, keepdims=True))
    return e / jnp.sum(e, axis=axis, keepdims=True)


def _top_blocks(imp, rounds):
    blk = lax.broadcasted_iota(jnp.int32, imp.shape, 0)
    sel = jnp.zeros(imp.shape, jnp.bool_)
    v = imp
    for _ in range(rounds):
        m = jnp.max(v, axis=0, keepdims=True)
        first = jnp.min(jnp.where(v == m, blk, imp.shape[0]), axis=0, keepdims=True)
        hit = blk == first
        sel = sel | hit
        v = jnp.where(hit, -jnp.inf, v)
    return sel


def _tile_heads(x, rep):
    return jnp.concatenate([x] * rep, axis=0)


def _nsa_prompt_kernel(q_ref, ka_ref, va_ref, win_ref, pool_ref, gate_ref, o_ref, *, tq, tk):
    qi = pl.program_id(1)
    nblk = pool_ref.shape[2]
    rep = NSA_REP
    rows = rep * tq
    start = qi * tq
    qpos4 = start + (lax.broadcasted_iota(jnp.int32, (rows, 1), 0) & (tq - 1))
    blk4 = lax.broadcasted_iota(jnp.int32, (rows, nblk), 1)
    blk_t = lax.broadcasted_iota(jnp.int32, (nblk, tq), 0)
    cur_t = (start + lax.broadcasted_iota(jnp.int32, (nblk, tq), 1)) // CMP_BLOCK
    blk_t4 = lax.broadcasted_iota(jnp.int32, (nblk, rows), 0)
    qpos_t4 = start + (lax.broadcasted_iota(jnp.int32, (nblk, rows), 1) & (tq - 1))
    forced = (blk_t == 0) | (blk_t == cur_t) | (blk_t == cur_t - 1)
    seen = blk_t <= cur_t
    own = (blk_t >= start // CMP_BLOCK) & (blk_t < (start + tq) // CMP_BLOCK)
    tri = (lax.broadcasted_iota(jnp.int32, (rows, tq), 1)
           <= (lax.broadcasted_iota(jnp.int32, (rows, tq), 0) & (tq - 1)))
    gates = gate_ref[0]

    ntile = WINDOW // tq + 1
    qs, pools, s_cmp, s_blk, s_win, win_tiles = [], [], [], [], [], []
    for g in range(NSA_GROUPS):
        q = q_ref[0, g * rep:(g + 1) * rep].reshape(rows, LANES)
        pool = pool_ref[0, g].astype(BF16)
        qs.append(q)
        pools.append(pool)
        s_cmp.append(_dot_nt(q, pool))
        s_blk.append(_dot_nt(pool, q))
        scores, tiles = [], []
        for c in range(ntile):
            idx = qi - (ntile - 1) + c
            sl = pl.ds(pl.multiple_of(jnp.maximum(idx, 0) * tq, tq), tq)
            kvw = win_ref[0, g, sl, :]
            s = _dot_nt(q, kvw)
            if c == ntile - 1:
                s = jnp.where(tri, s, NEG_INF)
            else:
                if c == 0:
                    s = jnp.where(tri, NEG_INF, s)
                s = s + jnp.where(idx >= 0, 0.0, NEG_INF)
            scores.append(s)
            tiles.append(kvw)
        s_win.append(jnp.concatenate(scores, axis=1))
        win_tiles.append(tiles)

    q_aug, o_cmp, o_win = [], [], []
    for g in range(NSA_GROUPS):
        q, pool, tiles = qs[g], pools[g], win_tiles[g]
        p_c = _masked_softmax(s_cmp[g], (blk4 + 1) * CMP_BLOCK - 1 <= qpos4, axis=1)
        o_cmp.append(_dot(p_c.astype(BF16), pool))
        p_t = _masked_softmax(s_blk[g], (blk_t4 + 1) * CMP_BLOCK - 1 <= qpos_t4, axis=0)
        imp = p_t[:, 0:tq]
        for r in range(1, rep):
            imp = imp + p_t[:, r * tq:(r + 1) * tq]
        imp = jnp.where(forced | ~seen, -jnp.inf, imp)
        sel = (forced | _top_blocks(imp, SEL_TOPK - 3)) & seen
        bias = jnp.where(sel & ~own, 0.0, NEG_INF).T
        q_aug.append(jnp.concatenate([q, _tile_heads(bias, rep).astype(BF16)], axis=1))
        s_w = s_win[g]
        e_w = jnp.exp2(s_w - jnp.max(s_w, axis=1, keepdims=True))
        l_w = jnp.sum(e_w, axis=1, keepdims=True)
        e_w = e_w.astype(BF16)
        o_w = _dot(e_w[:, 0:tq], tiles[0])
        for c in range(1, ntile):
            o_w = o_w + _dot(e_w[:, c * tq:(c + 1) * tq], tiles[c])
        o_win.append(o_w / l_w)

    def sel_step(j, carry):
        sl = pl.ds(pl.multiple_of(j * tk, tk), tk)
        scores = [_dot_nt(q_aug[g], ka_ref[0, g, sl, :]) for g in range(NSA_GROUPS)]
        return tuple(_flash_update(*carry[g], scores[g], va_ref[0, g, sl, :]) for g in range(NSA_GROUPS))

    init = tuple((jnp.full((rows, 1), NEG_INF, F32), jnp.zeros((rows, LANES), F32)) for _ in range(NSA_GROUPS))
    carry = lax.fori_loop(0, start // tk + 1, sel_step, init)
    own_keys = pl.ds(pl.multiple_of(start, tq), tq)
    scores = [jnp.where(tri, _dot_nt(qs[g], ka_ref[0, g, own_keys, 0:LANES]), NEG_INF) for g in range(NSA_GROUPS)]
    carry = tuple(_flash_update(*carry[g], scores[g], va_ref[0, g, own_keys, :]) for g in range(NSA_GROUPS))

    heads = []
    for g in range(NSA_GROUPS):
        acc = carry[g][1]
        o_s = acc / acc[:, 0:1]
        for r in range(rep):
            hh = g * rep + r
            rs = slice(r * tq, (r + 1) * tq)
            heads.append(gates[:, 3 * hh:3 * hh + 1] * o_cmp[g][rs] + gates[:, 3 * hh + 1:3 * hh + 2] * o_s[rs]
                         + gates[:, 3 * hh + 2:3 * hh + 3] * o_win[g][rs])
    low = lax.broadcasted_iota(jnp.int32, (tq, LANES), 1) < NSA_DIM
    slabs = [jnp.where(low, pltpu.roll(heads[2 * k], NSA_DIM, 1), heads[2 * k + 1]) for k in range(NSA_HEADS // 2)]
    o_ref[0] = jnp.concatenate(slabs, axis=1).astype(o_ref.dtype)


def _nsa_prompt(q, ka, va, win, pool, gates, tq, tk):
    nb, nh, ns, _ = q.shape
    nblk = ns // CMP_BLOCK
    assert WINDOW % tq == 0 and tk % tq == 0
    kern = functools.partial(_nsa_prompt_kernel, tq=tq, tk=tk)
    whole = lambda a: pl.BlockSpec((1,) + a.shape[1:], lambda b, i: (b, 0, 0, 0))
    return pl.pallas_call(
        kern, grid=(nb, ns // tq),
        in_specs=[pl.BlockSpec((1, nh, tq, LANES), lambda b, i: (b, 0, i, 0)),
                  whole(ka), whole(va), whole(win),
                  pl.BlockSpec((1, NSA_GROUPS, nblk, LANES), lambda b, i: (b, 0, 0, 0)),
                  pl.BlockSpec((1, tq, LANES), lambda b, i: (b, i, 0))],
        out_specs=pl.BlockSpec((1, tq, nh * NSA_DIM), lambda b, i: (b, i, 0)),
        out_shape=jax.ShapeDtypeStruct((nb, ns, nh * NSA_DIM), BF16),
        compiler_params=_params(("arbitrary", "arbitrary")), name="nsa_prompt",
    )(q, ka, va, win, pool, gates)


def _head_matmul_kernel(a_ref, b_ref, o_ref):
    o_ref[0] = _dot(a_ref[0], b_ref[0]).astype(o_ref.dtype)


def _head_matmul(a, b, dtype):
    nh, m, k = a.shape
    n = b.shape[2]
    return pl.pallas_call(
        _head_matmul_kernel, grid=(nh,),
        in_specs=[pl.BlockSpec((1, m, k), lambda h: (h, 0, 0)), pl.BlockSpec((1, k, n), lambda h: (h, 0, 0))],
        out_specs=pl.BlockSpec((1, m, n), lambda h: (h, 0, 0)),
        out_shape=jax.ShapeDtypeStruct((nh, m, n), dtype),
        compiler_params=_params(("arbitrary",)), name="head_matmul",
    )(a, b)


def _new_token_mask(rows, t_new):
    tok = lax.broadcasted_iota(jnp.int32, (rows, LANES), 0) % t_new
    lane = lax.broadcasted_iota(jnp.int32, (rows, LANES), 1)
    return (lane < t_new) & (lane <= tok)


def _fetch_pages(pt_ref, cache_ref, buf, sem, pages):
    b = pl.program_id(0)
    slot = b % 2

    def copy(seq, p, s):
        return pltpu.make_async_copy(cache_ref.at[pt_ref[seq, p]], buf.at[s, p], sem.at[s])

    def start(seq, s):
        def body(p, c):
            copy(seq, p, s).start()
            return c
        lax.fori_loop(0, pages, body, 0)

    @pl.when(b == 0)
    def _():
        start(0, 0)

    @pl.when(b + 1 < pl.num_programs(0))
    def _():
        start(b + 1, 1 - slot)

    def wait(p, c):
        copy(b, p, slot).wait()
        return c
    lax.fori_loop(0, pages, wait, 0)
    return slot


def _page_scratch(pages, feat):
    return [pltpu.VMEM((2, pages, feat, PAGE_SIZE), F32), pltpu.SemaphoreType.DMA((2,))]


def _mla_sample_kernel(pt_ref, q_ref, new_ref, cache_ref, o_ref, kv_scr, new_scr, m_scr, l_scr, acc_scr,
                       page_buf, page_sem, *, pages, t_new):
    slot = _fetch_pages(pt_ref, cache_ref, page_buf, page_sem, pages)
    j = pl.program_id(1)
    width = kv_scr.shape[0]

    @pl.when((pl.program_id(0) == 0) & (j == 0))
    def _():
        kv_scr[...] = jnp.zeros(kv_scr.shape, BF16)

    @pl.when(j == 0)
    def _():
        m0, l0, a0 = _flash_init(q_ref.shape[1], width)
        m_scr[...], l_scr[...], acc_scr[...] = m0, l0, a0

    for p in range(pages):
        kv_scr[0:MLA_ROW, p * PAGE_SIZE:(p + 1) * PAGE_SIZE] = page_buf[slot, p].astype(BF16)
    q = q_ref[0]
    kv = kv_scr[...]
    carry = _online_update((m_scr[...], l_scr[...], acc_scr[...]), _dot(q, kv), lambda p: _dot_nt(p, kv))
    m_scr[...], l_scr[...], acc_scr[...] = carry

    @pl.when(j == pl.num_programs(1) - 1)
    def _():
        new_scr[...] = jnp.zeros(new_scr.shape, BF16)
        new_scr[0:t_new, 0:MLA_ROW] = new_ref[0].astype(BF16)
        kvn = new_scr[...]
        s = jnp.where(_new_token_mask(q.shape[0], t_new), _dot_nt(q, kvn), NEG_INF)
        _, l, acc = _online_update(carry, s, lambda p: _dot(p, kvn))
        o_ref[0] = acc / l


def _mla_sample(page_table, q, new_rows, cache_t, pages):
    nb, rows, width = q.shape
    t_new = new_rows.shape[1]
    n_pages = page_table.shape[1]
    kern = functools.partial(_mla_sample_kernel, pages=pages, t_new=t_new)
    grid_spec = pltpu.PrefetchScalarGridSpec(
        num_scalar_prefetch=1, grid=(nb, n_pages // pages),
        in_specs=[pl.BlockSpec((1, rows, width), lambda b, j, pt: (b, 0, 0)),
                  pl.BlockSpec((1, t_new, MLA_ROW), lambda b, j, pt: (b, 0, 0)), pl.BlockSpec(memory_space=pl.ANY)],
        out_specs=pl.BlockSpec((1, rows, width), lambda b, j, pt: (b, 0, 0)),
        scratch_shapes=[pltpu.VMEM((width, pages * PAGE_SIZE), BF16), pltpu.VMEM((LANES, width), BF16),
                        pltpu.VMEM((rows, 1), F32), pltpu.VMEM((rows, 1), F32), pltpu.VMEM((rows, width), F32)]
        + _page_scratch(pages, MLA_ROW))
    return pl.pallas_call(
        kern, grid_spec=grid_spec, out_shape=jax.ShapeDtypeStruct((nb, rows, width), F32),
        compiler_params=_params(("arbitrary", "arbitrary")), name="mla_sample",
    )(page_table, q, new_rows, cache_t)


def _cmp_pool_kernel(pt_ref, w_ref, e_ref, cache_ref, o_ref, hi_scr, lo_scr, page_buf, page_sem, *, pages):
    slot = _fetch_pages(pt_ref, cache_ref, page_buf, page_sem, pages)
    w = w_ref[...]
    for p in range(pages):
        x = page_buf[slot, p] * w
        hi = x.astype(BF16)
        sl = slice(p * PAGE_SIZE, (p + 1) * PAGE_SIZE)
        hi_scr[:, sl] = hi
        lo_scr[:, sl] = (x - hi.astype(F32)).astype(BF16)
    e = e_ref[...]
    o_ref[0] = _dot_nt(e, hi_scr[...]) + _dot_nt(e, lo_scr[...])


def _cmp_pool(page_table, w_pool_t, cache_t, pages):
    nb, n_pages = page_table.shape
    feat = cache_t.shape[1]
    per_step = pages * PAGE_SIZE // CMP_BLOCK
    expand = _block_expand(per_step, pages * PAGE_SIZE)
    kern = functools.partial(_cmp_pool_kernel, pages=pages)
    grid_spec = pltpu.PrefetchScalarGridSpec(
        num_scalar_prefetch=1, grid=(nb, n_pages // pages),
        in_specs=[pl.BlockSpec(w_pool_t.shape, lambda b, j, pt: (0, 0)),
                  pl.BlockSpec(expand.shape, lambda b, j, pt: (0, 0)), pl.BlockSpec(memory_space=pl.ANY)],
        out_specs=pl.BlockSpec((1, per_step, feat), lambda b, j, pt: (b, j, 0)),
        scratch_shapes=[pltpu.VMEM((feat, pages * PAGE_SIZE), BF16), pltpu.VMEM((feat, pages * PAGE_SIZE), BF16)]
        + _page_scratch(pages, feat))
    return pl.pallas_call(
        kern, grid_spec=grid_spec,
        out_shape=jax.ShapeDtypeStruct((nb, n_pages * PAGE_SIZE // CMP_BLOCK, feat), F32),
        compiler_params=_params(("arbitrary", "arbitrary")), name="cmp_pool",
    )(page_table, w_pool_t, expand, cache_t)


def _pack_group(rows, g):
    ko, vo = g * NSA_DIM, NSA_GROUPS * NSA_DIM + g * NSA_DIM
    return jnp.concatenate([rows[:, ko:ko + NSA_DIM], rows[:, vo:vo + NSA_DIM]], axis=1).astype(BF16)


def _pack_group_t(feats, g):
    ko, vo = g * NSA_DIM, NSA_GROUPS * NSA_DIM + g * NSA_DIM
    return jnp.concatenate([feats[ko:ko + NSA_DIM], feats[vo:vo + NSA_DIM]], axis=0).astype(BF16)


def _nsa_sample_kernel(pt_ref, q_ref, pool_ref, gate_ref, win_ref, wnew_ref, snew_ref, expand_ref, cache_ref,
                       o_ref, kv_scr, new_scr, q_scr, sel_scr, oc_scr, ow_scr, m_scr, l_scr, acc_scr,
                       page_buf, page_sem, *, pages, t_new):
    slot = _fetch_pages(pt_ref, cache_ref, page_buf, page_sem, pages)
    j = pl.program_id(1)
    rep = NSA_REP
    rows = rep * t_new
    nblk = pool_ref.shape[1]
    tk = pages * PAGE_SIZE
    new_ok = _new_token_mask(rows, t_new)

    def new_tile(ref, g):
        new_scr[...] = jnp.zeros(new_scr.shape, BF16)
        new_scr[0:t_new, :] = _pack_group(ref[0], g)
        return new_scr[...]

    @pl.when(j == 0)
    def _():
        pool = pool_ref[0]
        win_t = win_ref[0]
        wb = win_t.shape[1]
        tok = lax.broadcasted_iota(jnp.int32, (rows, 1), 0) % t_new
        lane = lax.broadcasted_iota(jnp.int32, (nblk, LANES), 1)
        blk = lax.broadcasted_iota(jnp.int32, (nblk, LANES), 0)
        for g in range(NSA_GROUPS):
            q = q_ref[0, g * rows:(g + 1) * rows]
            pg = _pack_group(pool, g)
            oc_scr[g] = _dot(_softmax(_dot_nt(q, pg), axis=1).astype(BF16), pg)
            q_scr[...] = jnp.zeros(q_scr.shape, BF16)
            q_scr[0:rows, :] = q
            p_t = _softmax(_dot_nt(pg, q_scr[...]), axis=0)
            tot = p_t
            for r in range(1, rep):
                tot = tot + pltpu.roll(p_t, LANES - r * t_new, 1)
            tot = jnp.where(lane < t_new, tot, 0.0)
            imp = tot
            for r in range(1, rep):
                imp = imp + pltpu.roll(tot, r * t_new, 1)
            forced = (blk == 0) | (blk == nblk - 1)
            sel = forced | _top_blocks(jnp.where(forced, -jnp.inf, imp), SEL_TOPK - 3)
            sel_scr[g] = jnp.where(sel, 1.0, 0.0).T[0:rows]
            wg = _pack_group_t(win_t, g)
            i = lax.broadcasted_iota(jnp.int32, (1, wb), 1)
            ng = new_tile(wnew_ref, g)
            s_w = jnp.concatenate([jnp.where(i > tok + (wb - WINDOW), _dot(q, wg), NEG_INF),
                                   jnp.where(new_ok, _dot_nt(q, ng), NEG_INF)], axis=1)
            p_w = _softmax(s_w, axis=1).astype(BF16)
            ow_scr[g] = _dot_nt(p_w[:, :wb], wg) + _dot(p_w[:, wb:], ng)
            m0, l0, a0 = _flash_init(rows, LANES)
            m_scr[g], l_scr[g], acc_scr[g] = m0, l0, a0

    for p in range(pages):
        page = page_buf[slot, p]
        for g in range(NSA_GROUPS):
            kv_scr[g, :, p * PAGE_SIZE:(p + 1) * PAGE_SIZE] = _pack_group_t(page, g)
    sl = pl.ds(pl.multiple_of(j * tk, tk), tk)
    for g in range(NSA_GROUPS):
        q = q_ref[0, g * rows:(g + 1) * rows]
        kv = kv_scr[g]
        hit = _dot(sel_scr[g].astype(BF16), expand_ref[:, sl])
        s = _dot(q, kv) + (hit - 1.0) * (-NEG_INF)
        carry = _online_update((m_scr[g], l_scr[g], acc_scr[g]), s, lambda p, kv=kv: _dot_nt(p, kv))
        m_scr[g], l_scr[g], acc_scr[g] = carry

    @pl.when(j == pl.num_programs(1) - 1)
    def _():
        gates = gate_ref[0]
        for g in range(NSA_GROUPS):
            q = q_ref[0, g * rows:(g + 1) * rows]
            ng = new_tile(snew_ref, g)
            s = jnp.where(new_ok, _dot_nt(q, ng), NEG_INF)
            _, l, acc = _online_update((m_scr[g], l_scr[g], acc_scr[g]), s, lambda p, ng=ng: _dot(p, ng))
            gg = gates[g * rows:(g + 1) * rows]
            o_ref[0, g * rows:(g + 1) * rows, :] = (gg[:, 0:1] * oc_scr[g] + gg[:, 1:2] * (acc / l)
                                                    + gg[:, 2:3] * ow_scr[g])


def _nsa_sample(page_table, q, pool, gates, win_t, win_new, slc_new, expand, cache_t, pages):
    nb, rows2, _ = q.shape
    t_new = slc_new.shape[1]
    n_pages = page_table.shape[1]
    feat = cache_t.shape[1]
    rows = rows2 // NSA_GROUPS
    per_b = lambda shape: pl.BlockSpec((1,) + shape, lambda b, j, pt: (b, 0, 0))
    kern = functools.partial(_nsa_sample_kernel, pages=pages, t_new=t_new)
    grp = lambda *shape: pltpu.VMEM((NSA_GROUPS,) + shape, F32)
    grid_spec = pltpu.PrefetchScalarGridSpec(
        num_scalar_prefetch=1, grid=(nb, n_pages // pages),
        in_specs=[per_b((rows2, LANES)), per_b(pool.shape[1:]), per_b((rows2, LANES)), per_b(win_t.shape[1:]),
                  per_b((t_new, feat)), per_b((t_new, feat)),
                  pl.BlockSpec(expand.shape, lambda b, j, pt: (0, 0)), pl.BlockSpec(memory_space=pl.ANY)],
        out_specs=per_b((rows2, LANES)),
        scratch_shapes=[pltpu.VMEM((NSA_GROUPS, LANES, pages * PAGE_SIZE), BF16), pltpu.VMEM((LANES, LANES), BF16),
                        pltpu.VMEM((LANES, LANES), BF16), grp(rows, expand.shape[0]),
                        grp(rows, LANES), grp(rows, LANES), grp(rows, 1), grp(rows, 1), grp(rows, LANES)]
        + _page_scratch(pages, feat))
    return pl.pallas_call(
        kern, grid_spec=grid_spec, out_shape=jax.ShapeDtypeStruct((nb, rows2, LANES), F32),
        compiler_params=_params(("arbitrary", "arbitrary")), name="nsa_sample",
    )(page_table, q, pool, gates, win_t, win_new, slc_new, expand, cache_t)


def _tail_kernel(x_ref, ya_ref, yb_ref, ga_ref, gb_ref, p_ref, wba_ref, wbn_ref, wo_ref, lnf_ref,
                 wg_ref, wu_ref, wd_ref, lnp_ref, wpg_ref, wpp_ref, lnl_ref, o_ref, hid_scr, *, chunk):
    merged = ga_ref[...] * _dot(ya_ref[...], wba_ref[...]) + gb_ref[...] * _dot(yb_ref[...], wbn_ref[...])
    x = x_ref[...] + _dot(merged.astype(BF16), wo_ref[...])
    h = _rms(x, lnf_ref[...]).astype(BF16)

    def ffn(c, carry):
        sl = pl.ds(pl.multiple_of(c * chunk, chunk), chunk)
        hid_scr[:, sl] = (jax.nn.silu(_dot(h, wg_ref[:, sl])) * _dot(h, wu_ref[:, sl])).astype(BF16)
        return carry

    lax.fori_loop(0, wg_ref.shape[1] // chunk, ffn, 0)
    x = x + _dot(hid_scr[...], wd_ref[...])
    gate = jax.nn.sigmoid(_dot(_rms(x, lnp_ref[...]).astype(BF16), wpg_ref[...]))
    x = x + gate * _dot(p_ref[...].astype(BF16), wpp_ref[...])
    o_ref[...] = _rms(x, lnl_ref[...])


def _tail(x, ya, yb, ga, gb, p, weights, tm):
    n, d_model = x.shape
    row = lambda a: pl.BlockSpec((tm, a.shape[1]), lambda i: (i, 0))
    acts = (x, ya, yb, ga, gb, p)
    kern = functools.partial(_tail_kernel, chunk=2 * LANES)
    return pl.pallas_call(
        kern, grid=(n // tm,),
        in_specs=[row(a) for a in acts] + [_resident(w.shape) for w in weights],
        out_specs=pl.BlockSpec((tm, d_model), lambda i: (i, 0)),
        out_shape=jax.ShapeDtypeStruct((n, d_model), F32),
        scratch_shapes=[pltpu.VMEM((tm, weights[4].shape[1]), BF16)],
        compiler_params=_params(("arbitrary",)), name="tail",
    )(*acts, *weights)


def _rope_tables(pos):
    pos = pos.astype(F32)[:, None]
    n = pos.shape[0]

    def cs(rot):
        half = rot // 2
        inv = jnp.float32(ROPE_THETA) ** (-jnp.arange(half, dtype=F32) * 2.0 / rot)
        ang = pos * inv[None, :]
        return jnp.cos(ang), jnp.sin(ang)

    one = lambda w: jnp.ones((n, w), F32)
    zero = lambda w: jnp.zeros((n, w), F32)
    c, s = cs(MLA_ROPE)
    pad = LANES - MLA_NOPE - MLA_ROPE
    cm = jnp.concatenate([one(MLA_NOPE), c, c, one(pad)], axis=1)
    sm = jnp.concatenate([zero(MLA_NOPE), -s, s, zero(pad)], axis=1)
    c, s = cs(NSA_ROT)
    rest = NSA_DIM - NSA_ROT
    cn = jnp.concatenate([c, c, one(rest)] * (LANES // NSA_DIM), axis=1)
    sn = jnp.concatenate([-s, s, zero(rest)] * (LANES // NSA_DIM), axis=1)
    return cm, sm, cn, sn


def _layout_w_in(w_in):
    d_model = w_in.shape[0]
    sizes = (MLA_HEADS * (MLA_NOPE + MLA_ROPE), MLA_KV_LORA, MLA_ROPE, NSA_HEADS * NSA_DIM,
             SEG_KV // 3, SEG_KV // 3, SEG_KV // 3, 3 * NSA_HEADS, 2 * d_model)
    offs = np.concatenate([[0], np.cumsum(sizes)])
    qa, ca, ra, qb, zc, zs, zw, gn, gm = [w_in[:, int(offs[i]):int(offs[i + 1])] for i in range(len(sizes))]
    qa = qa.reshape(d_model, MLA_HEADS, MLA_NOPE + MLA_ROPE)
    qa = jnp.pad(qa, ((0, 0), (0, 0), (0, LANES - MLA_NOPE - MLA_ROPE))).reshape(d_model, SEG_QA)
    ra = jnp.pad(ra, ((0, 0), (MLA_NOPE, LANES - MLA_NOPE - MLA_ROPE)))
    gn = jnp.pad(gn, ((0, 0), (0, SEG_GN - gn.shape[1])))
    return jnp.concatenate([qa, ca, ra, qb, zc, zs, zw, gn, gm], axis=1).astype(BF16)


def _layout_w_up(w_uk, w_uv):
    wk = jnp.pad(w_uk, ((0, 0), (0, 0), (0, LANES - MLA_NOPE))).reshape(MLA_KV_LORA, SEG_QA)
    return jnp.concatenate([wk, w_uv.reshape(MLA_KV_LORA, MLA_HEADS * MLA_V)], axis=1).astype(BF16)


def _layout_absorb(w_uk, w_uv, width):
    to_lat = jnp.zeros((MLA_HEADS, LANES, width), F32)
    to_lat = to_lat.at[:, 0:MLA_NOPE, 0:MLA_KV_LORA].set(w_uk.transpose(1, 2, 0))
    to_lat = to_lat.at[:, MLA_NOPE:MLA_NOPE + MLA_ROPE, MLA_KV_LORA:MLA_ROW].set(jnp.eye(MLA_ROPE, dtype=F32))
    to_val = jnp.zeros((MLA_HEADS, width, MLA_V), F32).at[:, 0:MLA_KV_LORA, :].set(w_uv.transpose(1, 0, 2))
    return to_lat.astype(BF16), to_val.astype(BF16)


def _block_expand(nblk, nkeys):
    return (jnp.arange(nkeys)[None, :] // CMP_BLOCK == jnp.arange(nblk)[:, None]).astype(BF16)


def kernel(x_prompt, x_sample, cache_mla, cache_nsa_cmp, cache_nsa_slc, state_nsa_win, page_table, p_prompt, p_sample, ln_attn, w_in, mla_kv_norm, mla_w_uk, mla_w_uv, nsa_w_cmp_k, nsa_w_cmp_v, w_branch_mla, w_branch_nsa, w_out, ln_ffn, w_ffn_gate, w_ffn_up, w_ffn_down, ln_ple, w_ple_gate, w_ple_proj, ln_final):
    nb, ns, d_model = x_prompt.shape
    db, t_new, _ = x_sample.shape
    depth = w_in.shape[0]
    n_pages = page_table.shape[1]
    past = n_pages * PAGE_SIZE
    wb = state_nsa_win.shape[2]
    width = 2 * NSA_GROUPS * NSA_DIM
    g_, d_ = NSA_GROUPS, NSA_DIM
    assert depth == 1 and t_new <= CMP_BLOCK and past % CMP_BLOCK == 0 and wb == WINDOW and past >= wb
    assert ns % 512 == 0 and (db * t_new) % 512 == 0 and NSA_REP * t_new <= LANES
    tm = 512
    pages = n_pages
    assert pages <= 64
    lat_w = 3 * LANES
    i = 0

    w_all = _layout_w_in(w_in[i])
    w_up = _layout_w_up(mla_w_uk[i], mla_w_uv[i])
    to_lat, to_val = _layout_absorb(mla_w_uk[i], mla_w_uv[i], lat_w)
    w_pool = jnp.concatenate([nsa_w_cmp_k[i]] * g_ + [nsa_w_cmp_v[i]] * g_, axis=1)
    w_pool_t = jnp.tile(w_pool.T, (1, PAGE_SIZE // CMP_BLOCK))
    ln_a = ln_attn[i][None, :]
    kvn = mla_kv_norm[i][None, :]
    tail_w = (w_branch_mla[i].astype(BF16), w_branch_nsa[i].astype(BF16), w_out[i].astype(BF16),
              ln_ffn[i][None, :], w_ffn_gate[i].astype(BF16), w_ffn_up[i].astype(BF16),
              w_ffn_down[i].astype(BF16), ln_ple[i][None, :], w_ple_gate[i].astype(BF16),
              w_ple_proj[i].astype(BF16), ln_final[None, :])

    xp = x_prompt.reshape(nb * ns, d_model)
    nblk = ns // CMP_BLOCK
    (qm, km, vm, mla_p, qn, kvc_p, kvs_p, kvw_p, ka, va, win, pool, gn, ga, gb) = _proj(
        xp, _rope_tables(jnp.arange(ns)), ln_a, w_all, w_up, kvn, w_pool, nb, ns, tm, nblk)
    token_major = lambda a: a.transpose(0, 2, 1).reshape(-1, a.shape[1])
    mla_p, kvc_p, kvs_p, kvw_p = (token_major(a) for a in (mla_p, kvc_p, kvs_p, kvw_p))
    ya = _mla_flash(qm, km, vm, 512)
    pool5 = pool.reshape(nb, nblk, 2, g_, d_)
    pool_g = jnp.concatenate([pool5[:, :, 0].transpose(0, 2, 1, 3), pool5[:, :, 1].transpose(0, 2, 1, 3)], axis=-1)
    yb = _nsa_prompt(qn, ka, va, win, pool_g, gn.reshape(nb, ns, LANES), 128, 1024)
    y_prompt = _tail(xp, ya.reshape(nb * ns, -1), yb.reshape(nb * ns, -1), ga, gb,
                     p_prompt[i].reshape(nb * ns, -1), tail_w, tm).reshape(nb, ns, d_model)

    n_s = db * t_new
    xs = x_sample.reshape(n_s, d_model)
    pos_s = jnp.tile(past + jnp.arange(t_new), db)
    (qm, _, _, mla_s, qn, kvc_s, kvs_s, kvw_s, _, _, _, _, gn, ga, gb) = _proj(
        xs, _rope_tables(pos_s), ln_a, w_all, w_up, kvn, w_pool, 1, n_s, tm, LANES)
    mla_s, kvc_s, kvs_s, kvw_s = (token_major(a) for a in (mla_s, kvc_s, kvs_s, kvw_s))
    by_batch = lambda a: a.reshape(a.shape[0], db, t_new, a.shape[-1]).transpose(1, 0, 2, 3).reshape(
        db, a.shape[0] * t_new, a.shape[-1])
    q_lat = by_batch(_head_matmul(qm[0], to_lat, BF16))
    mla_t = cache_mla[i].transpose(0, 2, 1)
    o_lat = _mla_sample(page_table, q_lat, mla_s.reshape(db, t_new, MLA_ROW), mla_t, pages)
    o_lat = o_lat.reshape(db, MLA_HEADS, t_new, lat_w).transpose(1, 0, 2, 3).reshape(MLA_HEADS, n_s, lat_w)
    ya = _head_matmul(o_lat.astype(BF16), to_val, BF16).transpose(1, 0, 2).reshape(n_s, MLA_HEADS * MLA_V)

    cmp_t = cache_nsa_cmp[i].reshape(-1, PAGE_SIZE, width).transpose(0, 2, 1)
    slc_t = cache_nsa_slc[i].reshape(-1, PAGE_SIZE, width).transpose(0, 2, 1)
    pool_s = _cmp_pool(page_table, w_pool_t, cmp_t, pages)
    gate_s = gn[:, :3 * NSA_HEADS].reshape(db, t_new, NSA_HEADS, 3).transpose(0, 2, 1, 3).reshape(
        db, NSA_HEADS * t_new, 3)
    gate_s = jnp.pad(gate_s, ((0, 0), (0, 0), (0, LANES - 3)))
    win_state = state_nsa_win[i].reshape(db, wb, width)
    o_nsa = _nsa_sample(page_table, by_batch(qn[0]), pool_s, gate_s, win_state.transpose(0, 2, 1),
                        kvw_s.reshape(db, t_new, width), kvs_s.reshape(db, t_new, width),
                        _block_expand(past // CMP_BLOCK, past), slc_t, pages)
    yb = o_nsa[:, :, NSA_DIM:].reshape(db, NSA_HEADS, t_new, d_).transpose(0, 2, 1, 3).reshape(n_s, -1)
    y_sample = _tail(xs, ya, yb.astype(BF16), ga, gb, p_sample[i].reshape(n_s, -1), tail_w, tm).reshape(
        db, t_new, d_model)

    kv5 = lambda a, b, s: a.reshape(1, b, s, 2, g_, d_)
    win_all = jnp.concatenate([win_state, kvw_s.reshape(db, t_new, width)], axis=1)
    wkeep = min(WINDOW, ns)
    return (y_prompt, y_sample,
            mla_p.reshape(1, nb, ns, MLA_ROW), mla_s.reshape(1, db, t_new, MLA_ROW),
            kv5(kvc_p, nb, ns), kv5(kvc_s, db, t_new), kv5(kvs_p, nb, ns), kv5(kvs_s, db, t_new),
            kv5(kvw_p, nb, ns)[:, :, ns - wkeep:],
            kv5(win_all[:, -min(WINDOW, past + t_new):], db, min(WINDOW, past + t_new)))
```
